```python
import jax, jax.numpy as jnp
from jax import lax
import numpy as np

D_MODEL = 1024
BATCH = 8
SEQ = 2048
DEPTH = 4

GRID_W = 64
CTX_LEN = 256
EPS = 1e-6
NEG_BIG = -1e30
LB_FLOOR = 1e-30

N_BRANCH = 3
BRANCH_WIDTH = D_MODEL // 2

FOURIER_GROUPS = 4
FOURIER_WIDTH = BRANCH_WIDTH
FOURIER_GROUP_DIM = FOURIER_WIDTH // FOURIER_GROUPS

HEAD_DIM = 64
ATTN_HEADS = BRANCH_WIDTH // HEAD_DIM
ATTN_KV_HEADS = ATTN_HEADS // 4
ATTN_WIDTH = ATTN_HEADS * HEAD_DIM
KV_WIDTH = ATTN_KV_HEADS * HEAD_DIM
WINDOW = 128
BLOCK = 128
ROPE_THETA = 10000.0

HGRN_HEADS = 4
HGRN_DK = BRANCH_WIDTH // HGRN_HEADS
HGRN_DV = BRANCH_WIDTH // HGRN_HEADS
HGRN_KW = HGRN_HEADS * HGRN_DK
HGRN_VW = HGRN_HEADS * HGRN_DV
CHUNK = 64

D_FF = 4 * D_MODEL

_SEG_SIZES = (KV_WIDTH, KV_WIDTH, HGRN_VW, HGRN_KW, HGRN_KW,
              ATTN_WIDTH, HGRN_KW, HGRN_VW, FOURIER_WIDTH, N_BRANCH * D_MODEL)
IN_SPLITS = [sum(_SEG_SIZES[:i + 1]) for i in range(len(_SEG_SIZES) - 1)]
D_IN = sum(_SEG_SIZES)
N_STATE_COLS = sum(_SEG_SIZES[:5])
STATE_SPLITS = IN_SPLITS[:4]

kernel_name = "hybrid_fourier_swa_hgrn2_dit_prefix"

F32 = jnp.float32


def rms_norm(x, g):
    xf = x.astype(F32)
    y = xf * lax.rsqrt(jnp.mean(xf * xf, axis=-1, keepdims=True) + EPS)
    return (y * g.astype(F32)).astype(x.dtype)


def adaln_norm(x, g, shift, scale):
    return rms_norm(x, g) * (1 + scale) + shift


def modulation(cond, w, b):
    m = jax.nn.silu(cond) @ w + b
    return jnp.split(m, 6, axis=-1)


def _heads(a, n):
    return a.reshape(a.shape[:-1] + (n, a.shape[-1] // n))


def _to_bhtd(a, n):
    return jnp.swapaxes(_heads(a, n), 1, 2)


def _flip_t(a):
    return jnp.flip(a, axis=2)


def axial_rope_tables(n_tok):
    n_rows = n_tok // GRID_W
    row = jnp.repeat(jnp.arange(n_rows), GRID_W).astype(F32)
    col = jnp.tile(jnp.arange(GRID_W), n_rows).astype(F32)
    axis_dim = HEAD_DIM // 2
    inv_freq = ROPE_THETA ** (-jnp.arange(0, axis_dim, 2, dtype=F32) / axis_dim)
    ang_r = row[:, None] * inv_freq
    ang_c = col[:, None] * inv_freq
    return (jnp.cos(ang_r), jnp.sin(ang_r), jnp.cos(ang_c), jnp.sin(ang_c))


def apply_axial_rope(t, rope):
    cos_r, sin_r, cos_c, sin_c = rope

    def rot(u, cos, sin):
        u1, u2 = jnp.split(u, 2, axis=-1)
        cos = cos[None, :, None, :]
        sin = sin[None, :, None, :]
        return jnp.concatenate([u1 * cos - u2 * sin, u2 * cos + u1 * sin], axis=-1)

    tr, tc = jnp.split(t, 2, axis=-1)
    return jnp.concatenate([rot(tr, cos_r, sin_r), rot(tc, cos_c, sin_c)], axis=-1).astype(t.dtype)


def _sink_column(sink, ref):
    g = ATTN_HEADS // ATTN_KV_HEADS
    return jnp.broadcast_to(sink.astype(F32).reshape(ATTN_KV_HEADS, g, 1, 1), ref.shape[:-1] + (1,))


def window_attention(q, k, v, kc, vc, sink):
    b, s, _, dh = q.shape
    nb = s // BLOCK
    g = ATTN_HEADS // ATTN_KV_HEADS
    scale = dh ** -0.5
    qb = q.reshape(b, nb, BLOCK, ATTN_KV_HEADS, g, dh)
    pad = ((0, 0), (BLOCK, BLOCK), (0, 0), (0, 0))
    kp = jnp.pad(k, pad).reshape(b, nb + 2, BLOCK, ATTN_KV_HEADS, dh)
    vp = jnp.pad(v, pad).reshape(b, nb + 2, BLOCK, ATTN_KV_HEADS, dh)
    kw = jnp.concatenate([kp[:, :-2], kp[:, 1:-1], kp[:, 2:]], axis=2)
    vw = jnp.concatenate([vp[:, :-2], vp[:, 1:-1], vp[:, 2:]], axis=2)
    s_win = jnp.einsum('bnqhgd,bnkhd->bnhgqk', qb, kw).astype(F32) * scale
    s_ctx = jnp.einsum('bnqhgd,bchd->bnhgqc', qb, kc).astype(F32) * scale
    t_pos = jnp.arange(nb)[:, None, None] * BLOCK + jnp.arange(BLOCK)[None, :, None]
    k_pos = jnp.arange(nb)[:, None, None] * BLOCK - BLOCK + jnp.arange(3 * BLOCK)[None, None, :]
    valid = (jnp.abs(k_pos - t_pos) <= WINDOW) & (k_pos >= 0) & (k_pos < s)
    s_win = jnp.where(valid[None, :, None, None], s_win, NEG_BIG)
    logits = jnp.concatenate([s_win, s_ctx, _sink_column(sink, s_win)], axis=-1)
    p = jax.nn.softmax(logits, axis=-1)
    nw = 3 * BLOCK
    nc = kc.shape[1]
    o = (jnp.einsum('bnhgqk,bnkhd->bnqhgd', p[..., :nw].astype(v.dtype), vw)
         + jnp.einsum('bnhgqc,bchd->bnqhgd', p[..., nw:nw + nc].astype(v.dtype), vc))
    return o.reshape(b, s, ATTN_HEADS * dh)


def context_attention(qc, kc, vc, sink):
    b, lc, _, dh = qc.shape
    g = ATTN_HEADS // ATTN_KV_HEADS
    qg = qc.reshape(b, lc, ATTN_KV_HEADS, g, dh)
    s = jnp.einsum('bqhgd,bkhd->bhgqk', qg, kc).astype(F32) * dh ** -0.5
    p = jax.nn.softmax(jnp.concatenate([s, _sink_column(sink, s)], axis=-1), axis=-1)
    o = jnp.einsum('bhgqk,bkhd->bqhgd', p[..., :lc].astype(vc.dtype), vc)
    return o.reshape(b, lc, ATTN_HEADS * dh)


def fourier_mix(a):
    b, t, _ = a.shape
    ag = a.astype(F32).reshape(b, t, FOURIER_GROUPS, FOURIER_GROUP_DIM)
    y = jnp.fft.fft2(ag, axes=(1, 3), norm='ortho').real
    return y.reshape(b, t, FOURIER_WIDTH).astype(a.dtype)


def hgrn_forget(z, lb):
    zf = z.astype(F32)
    lbf = lb.astype(F32)
    k = (1.0 - lbf) * jax.nn.sigmoid(-zf)
    logf = jnp.logaddexp(jax.nn.log_sigmoid(zf),
                         jnp.log(jnp.maximum(lbf, LB_FLOOR)) + jax.nn.log_sigmoid(-zf))
    return k, logf


def hgrn_kv(i_raw, ff_raw, fb_raw, lb):
    v = _to_bhtd(i_raw.astype(F32), HGRN_HEADS)
    kf, lff = hgrn_forget(ff_raw, lb[0])
    kb, lfb = hgrn_forget(fb_raw, lb[1])
    return (v, _to_bhtd(kf, HGRN_HEADS), _to_bhtd(lff, HGRN_HEADS),
            _to_bhtd(kb, HGRN_HEADS), _to_bhtd(lfb, HGRN_HEADS))


def hgrn_q(q_raw):
    return _to_bhtd(jax.nn.silu(q_raw).astype(F32), HGRN_HEADS)


def gla_chunked(q, k, v, logf, s0):
    b, h, t, dk = q.shape
    dv = v.shape[-1]
    n = t // CHUNK

    def to_chunks(a):
        return jnp.moveaxis(a.reshape(b, h, n, CHUNK, a.shape[-1]), 2, 0)

    lower = jnp.tril(jnp.ones((CHUNK, CHUNK), dtype=bool))[:, :, None]

    def step(state, inp):
        qc, kc, vc, lc = inp
        g = jnp.cumsum(lc, axis=2)
        o_inter = jnp.einsum('bhtk,bhkv->bhtv', qc * jnp.exp(g), state)
        diff = g[:, :, :, None, :] - g[:, :, None, :, :]
        decay = jnp.where(lower, jnp.exp(jnp.where(lower, diff, 0.0)), 0.0)
        a = jnp.einsum('bhtsk,bhsk->bhts', qc[:, :, :, None, :] * decay, kc)
        o = o_inter + jnp.einsum('bhts,bhsv->bhtv', a, vc)
        g_last = g[:, :, -1:, :]
        new_state = (jnp.exp(g_last[:, :, 0, :])[..., None] * state
                     + jnp.einsum('bhsk,bhsv->bhkv', kc * jnp.exp(g_last - g), vc))
        return new_state, o

    s_fin, o = lax.scan(step, s0, (to_chunks(q), to_chunks(k), to_chunks(v), to_chunks(logf)))
    o = jnp.moveaxis(o, 0, 2).reshape(b, h, t, dv)
    return o, s_fin


def gla_final_state(k, v, logf):
    g = jnp.cumsum(logf, axis=2)
    return jnp.einsum('bhsk,bhsv->bhkv', k * jnp.exp(g[:, :, -1:, :] - g), v)


def hgrn_bidir(q, kf, lff, kb, lfb, v, s_f, s_b):
    o_f, s_f_new = gla_chunked(q, kf, v, lff, s_f)
    o_b, s_b_new = gla_chunked(_flip_t(q), _flip_t(kb), _flip_t(v), _flip_t(lfb), s_b)
    return o_f + _flip_t(o_b), s_f_new, s_b_new


def hgrn_readout(o, g_raw, norm_g):
    o = rms_norm(jnp.swapaxes(o, 1, 2), norm_g)
    return o.reshape(g_raw.shape).astype(g_raw.dtype) * jax.nn.silu(g_raw)


def merge_branches(o_four, o_attn, o_hgrn, gate_raw, w_branch, w_out):
    br = jnp.stack([o_four, o_attn, o_hgrn], axis=-2)
    proj = jnp.einsum('btnc,ncd->btnd', br, w_branch)
    gates = jax.nn.sigmoid(gate_raw.reshape(gate_raw.shape[:-1] + (N_BRANCH, D_MODEL)))
    return jnp.sum(gates * proj, axis=-2) @ w_out


def sq_relu_mlp(h, w1, w2):
    return jnp.square(jax.nn.relu(h @ w1)) @ w2


def setup_inputs(seed: int = 0) -> dict:
    key = jax.random.key(seed)
    ks = jax.random.split(key, 18)

    def nrm(k, shape, s):
        return jax.random.normal(k, shape, F32) * s

    return {
        "x": nrm(ks[0], (BATCH, SEQ, D_MODEL), 1.0),
        "c": nrm(ks[1], (BATCH, D_MODEL), 1.0),
        "ctx": nrm(ks[2], (BATCH, CTX_LEN, D_MODEL), 1.0),
        "c_ctx": nrm(ks[3], (D_MODEL,), 1.0),
        "w_mod": nrm(ks[4], (DEPTH, D_MODEL, 6 * D_MODEL), 0.5 * D_MODEL ** -0.5),
        "b_mod": nrm(ks[5], (DEPTH, 6 * D_MODEL), 0.02),
        "norm1_g": 1.0 + nrm(ks[6], (DEPTH, D_MODEL), 0.02),
        "norm2_g": 1.0 + nrm(ks[7], (DEPTH, D_MODEL), 0.02),
        "w_in": nrm(ks[8], (DEPTH, D_MODEL, D_IN), D_MODEL ** -0.5),
        "q_norm_g": 1.0 + nrm(ks[9], (DEPTH, HEAD_DIM), 0.02),
        "k_norm_g": 1.0 + nrm(ks[10], (DEPTH, HEAD_DIM), 0.02),
        "attn_sink": nrm(ks[11], (DEPTH, ATTN_HEADS), 0.5),
        "hgrn_lb_logits": nrm(ks[12], (DEPTH, 2, HGRN_KW), 0.5),
        "hgrn_norm_g": 1.0 + nrm(ks[13], (DEPTH, HGRN_DV), 0.02),
        "w_branch": nrm(ks[14], (DEPTH, N_BRANCH, BRANCH_WIDTH, D_MODEL), BRANCH_WIDTH ** -0.5),
        "w_out": nrm(ks[15], (DEPTH, D_MODEL, D_MODEL), D_MODEL ** -0.5),
        "w_ff1": nrm(ks[16], (DEPTH, D_MODEL, D_FF), D_MODEL ** -0.5),
        "w_ff2": nrm(ks[17], (DEPTH, D_FF, D_MODEL), D_FF ** -0.5),
    }


def reference(x, c, ctx, c_ctx, w_mod, b_mod, norm1_g, norm2_g, w_in, q_norm_g, k_norm_g,
              attn_sink, hgrn_lb_logits, hgrn_norm_g, w_branch, w_out, w_ff1, w_ff2):
    bsz, n_tok, _ = x.shape
    rope = axial_rope_tables(n_tok)
    lb_p = jax.nn.softmax(hgrn_lb_logits.astype(F32), axis=0)
    lb_all = jnp.cumsum(lb_p, axis=0) - lb_p[0:1]
    xc = ctx
    for l in range(DEPTH):
        last = l == DEPTH - 1
        mx = [m[:, None, :] for m in modulation(c, w_mod[l], b_mod[l])]
        mc = modulation(c_ctx, w_mod[l], b_mod[l])
        h = adaln_norm(x, norm1_g[l], mx[0], mx[1])
        hc = adaln_norm(xc, norm1_g[l], mc[0], mc[1])

        if last:
            kc_raw, vc_raw, ic_raw, fcf_raw, fcb_raw = jnp.split(
                hc @ w_in[l][:, :N_STATE_COLS], STATE_SPLITS, axis=-1)
        else:
            (kc_raw, vc_raw, ic_raw, fcf_raw, fcb_raw, qc_raw, qhc_raw, ghc_raw,
             fourc_raw, gatec_raw) = jnp.split(hc @ w_in[l], IN_SPLITS, axis=-1)
        kc = rms_norm(_heads(kc_raw, ATTN_KV_HEADS), k_norm_g[l])
        vc = _heads(vc_raw, ATTN_KV_HEADS)
        vhc, kfc, lffc, kbc, lfbc = hgrn_kv(ic_raw, fcf_raw, fcb_raw, lb_all[l])
        if last:
            s_f = gla_final_state(kfc, vhc, lffc)
            s_b = gla_final_state(_flip_t(kbc), _flip_t(vhc), _flip_t(lfbc))
        else:
            zeros = jnp.zeros((bsz, HGRN_HEADS, HGRN_DK, HGRN_DV), F32)
            oc_h, s_f, s_b = hgrn_bidir(hgrn_q(qhc_raw), kfc, lffc, kbc, lfbc, vhc, zeros, zeros)
            qc = rms_norm(_heads(qc_raw, ATTN_HEADS), q_norm_g[l])
            oc_a = context_attention(qc, kc, vc, attn_sink[l])
            oc_f = fourier_mix(fourc_raw)
            oc_hr = hgrn_readout(oc_h, ghc_raw, hgrn_norm_g[l])
            yc = merge_branches(oc_f, oc_a, oc_hr, gatec_raw, w_branch[l], w_out[l])
            xc = xc + mc[2] * yc
            hc2 = adaln_norm(xc, norm2_g[l], mc[3], mc[4])
            xc = xc + mc[5] * sq_relu_mlp(hc2, w_ff1[l], w_ff2[l])

        (k_raw, v_raw, i_raw, ff_raw, fb_raw, q_raw, qh_raw, gh_raw,
         four_raw, gate_raw) = jnp.split(h @ w_in[l], IN_SPLITS, axis=-1)
        q = apply_axial_rope(rms_norm(_heads(q_raw, ATTN_HEADS), q_norm_g[l]), rope)
        k = apply_axial_rope(rms_norm(_heads(k_raw, ATTN_KV_HEADS), k_norm_g[l]), rope)
        v = _heads(v_raw, ATTN_KV_HEADS)
        o_a = window_attention(q, k, v, kc, vc, attn_sink[l])
        vh, kf, lff, kb, lfb = hgrn_kv(i_raw, ff_raw, fb_raw, lb_all[l])
        o_h, _, _ = hgrn_bidir(hgrn_q(qh_raw), kf, lff, kb, lfb, vh, s_f, s_b)
        o_hr = hgrn_readout(o_h, gh_raw, hgrn_norm_g[l])
        o_f = fourier_mix(four_raw)
        y = merge_branches(o_f, o_a, o_hr, gate_raw, w_branch[l], w_out[l])
        x = x + mx[2] * y
        h2 = adaln_norm(x, norm2_g[l], mx[3], mx[4])
        x = x + mx[5] * sq_relu_mlp(h2, w_ff1[l], w_ff2[l])
    return x
```

```python
import functools
import math

import jax
import jax.numpy as jnp
from jax import lax
from jax.experimental import pallas as pl
from jax.experimental.pallas import tpu as pltpu

F32 = jnp.float32
BF16 = jnp.bfloat16

D_MODEL = 1024
SEQ = 2048
CTX_LEN = 256
TOKENS = SEQ + CTX_LEN
GRID_W = 64
EPS = 1e-6
NEG_BIG = -1e30
LB_FLOOR = 1e-30
ROPE_THETA = 10000.0

BRANCH_WIDTH = D_MODEL // 2
HEAD_DIM = 64
ATTN_HEADS = BRANCH_WIDTH // HEAD_DIM
ATTN_KV_HEADS = ATTN_HEADS // 4
KV_WIDTH = ATTN_KV_HEADS * HEAD_DIM
ATTN_BLOCK = 128
FOURIER_GROUPS = 4
FOURIER_GROUP_DIM = BRANCH_WIDTH // FOURIER_GROUPS
HGRN_HEADS = 4
HGRN_DK = BRANCH_WIDTH // HGRN_HEADS
D_FF = 4 * D_MODEL
N_BRANCH = 3

_SEG = (KV_WIDTH, KV_WIDTH, BRANCH_WIDTH, BRANCH_WIDTH, BRANCH_WIDTH,
        BRANCH_WIDTH, BRANCH_WIDTH, BRANCH_WIDTH, BRANCH_WIDTH, N_BRANCH * D_MODEL)
_OFF = [sum(_SEG[:i]) for i in range(len(_SEG) + 1)]
D_IN = _OFF[-1]

LANES = 128
VMEM_LIMIT_BYTES = 56 * 1024 * 1024

ROW_TILE = 256
ROW_TILES = TOKENS // ROW_TILE
LATENT_ROW_TILES = SEQ // ROW_TILE
HGRN_CHUNK = 64
HGRN_CHUNKS = TOKENS // HGRN_CHUNK
HGRN_CTX_CHUNKS = CTX_LEN // HGRN_CHUNK
EXP_CLAMP = 80.0
MOD_ROWS = 16


def _const_spec(shape):
    nd = len(shape)
    return pl.BlockSpec(shape, lambda *_: (0,) * nd, pipeline_mode=pl.Buffered(1))


def _params(semantics):
    return pltpu.CompilerParams(dimension_semantics=semantics, vmem_limit_bytes=VMEM_LIMIT_BYTES)


def _dot(a, b):
    return jnp.dot(a, b, preferred_element_type=F32)


def _dot_nt(a, b):
    return lax.dot_general(a, b, (((1,), (1,)), ((), ())), preferred_element_type=F32)


def _dot_tn(a, b):
    return lax.dot_general(a, b, (((0,), (0,)), ((), ())), preferred_element_type=F32)


def _sigmoid(x):
    return 1.0 / (1.0 + jnp.exp(-x))


def _silu(x):
    return x * _sigmoid(x)


def _mod_row_index(b, t):
    return jnp.where(t < LATENT_ROW_TILES, b, MOD_ROWS // 2)


MOD_COL_TILE = 1536


def _modulation_kernel(cond_ref, w_ref, b_ref, out_ref):
    a = _silu(cond_ref[...]).astype(BF16)
    out_ref[...] = _dot(a, w_ref[...].astype(BF16)) + b_ref[...]


def _modulation(cond, w_mod, b_mod):
    depth = w_mod.shape[0]
    n = w_mod.shape[2]
    return pl.pallas_call(
        _modulation_kernel,
        out_shape=jax.ShapeDtypeStruct((depth, MOD_ROWS, n), F32),
        grid=(depth, n // MOD_COL_TILE),
        in_specs=[
            pl.BlockSpec((MOD_ROWS, D_MODEL), lambda l, j: (0, 0)),
            pl.BlockSpec((None, D_MODEL, MOD_COL_TILE), lambda l, j: (l, 0, j)),
            pl.BlockSpec((None, 1, MOD_COL_TILE), lambda l, j: (l, 0, j)),
        ],
        out_specs=pl.BlockSpec((None, MOD_ROWS, MOD_COL_TILE), lambda l, j: (l, 0, j)),
        compiler_params=_params(("parallel", "parallel")),
        name="modulation",
    )(cond, w_mod, b_mod.reshape(depth, 1, n))


def _lane_index(shape):
    return lax.broadcasted_iota(jnp.int32, shape, len(shape) - 1)


def _head_sums(sq):
    shift = HEAD_DIM.bit_length() - 1
    r = lax.broadcasted_iota(jnp.int32, (LANES, LANES), 0) >> shift
    c = lax.broadcasted_iota(jnp.int32, (LANES, LANES), 1) >> shift
    ones_bd = jnp.where(r == c, 1.0, 0.0).astype(BF16)
    hi = sq.astype(BF16)
    lo = (sq - hi.astype(F32)).astype(BF16)
    return _dot(hi, ones_bd) + _dot(lo, ones_bd)


def _head_rms_norm(u, gain):
    ms = _head_sums(u * u) * (1.0 / HEAD_DIM)
    return u * lax.rsqrt(ms + EPS) * gain


def _rope(u, cos, sin_signed):
    first = (_lane_index(u.shape) & 31) < 16
    partner = jnp.where(first, pltpu.roll(u, LANES - 16, axis=1), pltpu.roll(u, 16, axis=1))
    return u * cos + partner * sin_signed


def _log_sigmoid(z):
    return jnp.minimum(z, 0.0) - jnp.log1p(jnp.exp(-jnp.abs(z)))


def _hgrn_lower_bound(lb_ref, layer, direction):
    depth = lb_ref.shape[0] // 2
    rows = [lb_ref[2 * j + direction:2 * j + direction + 1, :] for j in range(depth)]
    m = functools.reduce(jnp.maximum, rows)
    e = [jnp.exp(r - m) for r in rows]
    z = functools.reduce(lambda a, b: a + b, e)
    p = [x / z for x in e]
    csum = functools.reduce(lambda a, b: a + b, p[:layer + 1])
    return csum - p[0]


def _hgrn_forget(z, lb):
    lsp = _log_sigmoid(z)
    lsn = lsp - z
    k = (1.0 - lb) * jnp.exp(lsn)
    a = lsp
    b = jnp.log(jnp.maximum(lb, LB_FLOOR)) + lsn
    logf = jnp.maximum(a, b) + jnp.log1p(jnp.exp(-jnp.abs(a - b)))
    return k, logf


def _inproj_kernel(x_ref, mod_ref, g1_ref, w_ref, qg_ref, kg_ref, cos_ref, sin_ref, lb_ref,
                   q_ref, k_ref, v_ref, hv_ref, hk_ref, hlf_ref, hq_ref, hg_ref, four_ref, gate_ref,
                   *, layer):
    x = x_ref[0]
    ms = jnp.mean(x * x, axis=-1, keepdims=True)
    y = x * lax.rsqrt(ms + EPS) * g1_ref[...]
    shift = mod_ref[:, 0:D_MODEL]
    scale = mod_ref[:, D_MODEL:2 * D_MODEL]
    h = (y * (1.0 + scale) + shift).astype(BF16)

    def proj(seg, width=None):
        a = _OFF[seg]
        b = _OFF[seg + 1] if width is None else a + width
        return _dot(h, w_ref[:, a:b])

    cos = cos_ref[...]
    sin = sin_ref[...]
    low_head = _lane_index((ROW_TILE, LANES)) < HEAD_DIM

    kv = proj(0, 2 * KV_WIDTH)
    kr = _rope(_head_rms_norm(kv[:, :KV_WIDTH], kg_ref[...]), cos, sin)
    kr_sw = pltpu.roll(kr, HEAD_DIM, axis=1)
    k_ref[0, 0] = jnp.where(low_head, kr, kr_sw).astype(BF16)
    k_ref[0, 1] = jnp.where(low_head, kr_sw, kr).astype(BF16)
    vr = kv[:, KV_WIDTH:]
    vr_sw = pltpu.roll(vr, HEAD_DIM, axis=1)
    v_ref[0, 0] = jnp.where(low_head, vr, vr_sw).astype(BF16)
    v_ref[0, 1] = jnp.where(low_head, vr_sw, vr).astype(BF16)

    qr = proj(5)
    for c in range(BRANCH_WIDTH // LANES):
        u = qr[:, c * LANES:(c + 1) * LANES]
        q_ref[0, :, c * LANES:(c + 1) * LANES] = _rope(
            _head_rms_norm(u, qg_ref[...]), cos, sin).astype(BF16)

    hv_ref[0] = proj(2).astype(BF16)
    for direction, seg in ((0, 3), (1, 4)):
        lb = _hgrn_lower_bound(lb_ref, layer, direction)
        k, logf = _hgrn_forget(proj(seg), lb)
        hk_ref[0, :, direction * BRANCH_WIDTH:(direction + 1) * BRANCH_WIDTH] = k.astype(BF16)
        hlf_ref[0, :, direction * BRANCH_WIDTH:(direction + 1) * BRANCH_WIDTH] = logf
    hq_ref[0] = _silu(proj(6)).astype(BF16)
    hg_ref[0] = proj(7).astype(BF16)

    four_ref[0] = proj(8).astype(BF16)
    gate_ref[0] = proj(9).astype(BF16)


def _inproj(xs, mods_l, g1, w_in_l, qg, kg, cos, sin, lb_logits, *, layer):
    bsz = xs.shape[0]
    row = lambda w: pl.BlockSpec((1, ROW_TILE, w), lambda b, t: (b, t, 0))
    dup = pl.BlockSpec((1, ATTN_KV_HEADS, ROW_TILE, LANES), lambda b, t: (b, 0, t, 0))
    tok = lambda w, dt: jax.ShapeDtypeStruct((bsz, TOKENS, w), dt)
    dup_shape = jax.ShapeDtypeStruct((bsz, ATTN_KV_HEADS, TOKENS, LANES), BF16)
    return pl.pallas_call(
        functools.partial(_inproj_kernel, layer=layer),
        out_shape=(tok(BRANCH_WIDTH, BF16), dup_shape, dup_shape,
                   tok(BRANCH_WIDTH, BF16), tok(2 * BRANCH_WIDTH, BF16), tok(2 * BRANCH_WIDTH, F32),
                   tok(BRANCH_WIDTH, BF16), tok(BRANCH_WIDTH, BF16), tok(BRANCH_WIDTH, BF16),
                   tok(N_BRANCH * D_MODEL, BF16)),
        grid=(bsz, ROW_TILES),
        in_specs=[
            row(D_MODEL),
            pl.BlockSpec((None, 1, 6 * D_MODEL), lambda b, t: (_mod_row_index(b, t), 0, 0)),
            _const_spec((1, D_MODEL)),
            _const_spec((D_MODEL, D_IN)),
            _const_spec((1, LANES)),
            _const_spec((1, LANES)),
            pl.BlockSpec((ROW_TILE, LANES), lambda b, t: (t, 0)),
            pl.BlockSpec((ROW_TILE, LANES), lambda b, t: (t, 0)),
            _const_spec(lb_logits.shape),
        ],
        out_specs=(row(BRANCH_WIDTH), dup, dup,
                   row(BRANCH_WIDTH), row(2 * BRANCH_WIDTH), row(2 * BRANCH_WIDTH),
                   row(BRANCH_WIDTH), row(BRANCH_WIDTH), row(BRANCH_WIDTH),
                   row(N_BRANCH * D_MODEL)),
        compiler_params=_params(("parallel", "parallel")),
        name="inproj",
    )(xs, mods_l, g1, w_in_l, qg, kg, cos, sin, lb_logits)


ATTN_BLOCKS = TOKENS // ATTN_BLOCK
LATENT_ATTN_BLOCKS = SEQ // ATTN_BLOCK
WINDOW_KEYS = 3 * ATTN_BLOCK
GROUP = ATTN_HEADS // ATTN_KV_HEADS


def _attention_kernel(sink_ref, q_ref, kp_ref, kc_ref, kn_ref, kx_ref, vp_ref, vc_ref, vn_ref, vx_ref,
                      o_ref):
    i = pl.program_id(1)
    latent = i < LATENT_ATTN_BLOCKS
    prev_ok = jnp.logical_and(latent, i > 0).astype(jnp.int32)
    cur_ok = latent.astype(jnp.int32)
    next_ok = jnp.logical_and(latent, i < LATENT_ATTN_BLOCKS - 1).astype(jnp.int32)

    rows = GROUP * ATTN_BLOCK
    n_keys = WINDOW_KEYS + CTX_LEN
    t = lax.broadcasted_iota(jnp.int32, (rows, n_keys), 0) & (ATTN_BLOCK - 1)
    j = lax.broadcasted_iota(jnp.int32, (rows, n_keys), 1)
    block_ok = jnp.where(j < ATTN_BLOCK, prev_ok, jnp.where(j < 2 * ATTN_BLOCK, cur_ok, next_ok))
    in_window = jnp.logical_and(jnp.logical_and(j >= t, j <= t + 2 * ATTN_BLOCK), block_ok > 0)
    valid = jnp.logical_or(j >= WINDOW_KEYS, in_window)

    r = lax.broadcasted_iota(jnp.int32, (rows, 1), 0) >> (ATTN_BLOCK.bit_length() - 1)
    low_head = _lane_index((ATTN_BLOCK, LANES)) < HEAD_DIM
    zero = jnp.zeros((), BF16)

    for hk in range(ATTN_KV_HEADS):
        tiles = [q_ref[0, :, (2 * hk + c) * LANES:(2 * hk + c + 1) * LANES] for c in range(2)]
        qs = jnp.concatenate([jnp.where(low_head, tiles[0], zero), jnp.where(low_head, zero, tiles[0]),
                              jnp.where(low_head, tiles[1], zero), jnp.where(low_head, zero, tiles[1])],
                             axis=0)
        keys = jnp.concatenate([kp_ref[0, hk], kc_ref[0, hk], kn_ref[0, hk], kx_ref[0, hk]], axis=0)
        vals = jnp.concatenate([vp_ref[0, hk], vc_ref[0, hk], vn_ref[0, hk], vx_ref[0, hk]], axis=0)
        s = _dot_nt(qs, keys) * (HEAD_DIM ** -0.5)
        s = jnp.where(valid, s, NEG_BIG)
        sink = jnp.full((rows, 1), sink_ref[GROUP * hk], F32)
        for g in range(1, GROUP):
            sink = jnp.where(r == g, sink_ref[GROUP * hk + g], sink)
        m = jnp.maximum(jnp.max(s, axis=-1, keepdims=True), sink)
        p = jnp.exp(s - m)
        denom = jnp.sum(p, axis=-1, keepdims=True) + jnp.exp(sink - m)
        o = _dot(p.astype(BF16), vals) / denom
        for c in range(2):
            pair = jnp.where(low_head, o[(2 * c) * ATTN_BLOCK:(2 * c + 1) * ATTN_BLOCK],
                             o[(2 * c + 1) * ATTN_BLOCK:(2 * c + 2) * ATTN_BLOCK])
            o_ref[0, :, (2 * hk + c) * LANES:(2 * hk + c + 1) * LANES] = pair.astype(BF16)


def _attention(q, k_dup, v_dup, sink):
    bsz = q.shape[0]
    last = LATENT_ATTN_BLOCKS - 1

    def win(offset):
        return pl.BlockSpec((1, ATTN_KV_HEADS, ATTN_BLOCK, LANES),
                            lambda b, i: (b, 0, jnp.clip(i + offset, 0, last), 0))

    ctx = pl.BlockSpec((1, ATTN_KV_HEADS, CTX_LEN, LANES), lambda b, i: (b, 0, SEQ // CTX_LEN, 0))
    return pl.pallas_call(
        _attention_kernel,
        out_shape=jax.ShapeDtypeStruct((bsz, TOKENS, BRANCH_WIDTH), BF16),
        grid=(bsz, ATTN_BLOCKS),
        in_specs=[pl.BlockSpec(memory_space=pltpu.SMEM),
                  pl.BlockSpec((1, ATTN_BLOCK, BRANCH_WIDTH), lambda b, i: (b, i, 0)),
                  win(-1), win(0), win(1), ctx, win(-1), win(0), win(1), ctx],
        out_specs=pl.BlockSpec((1, ATTN_BLOCK, BRANCH_WIDTH), lambda b, i: (b, i, 0)),
        compiler_params=_params(("parallel", "parallel")),
        name="attention",
    )(sink, q, k_dup, k_dup, k_dup, k_dup, v_dup, v_dup, v_dup, v_dup)


def _cumsum_rows(tri, x):
    hi = x.astype(BF16)
    r1 = x - hi.astype(F32)
    mid = r1.astype(BF16)
    lo = (r1 - mid.astype(F32)).astype(BF16)
    t = tri.astype(BF16)
    return _dot(t, hi) + _dot(t, mid) + _dot(t, lo)


def _safe_exp(x):
    return jnp.exp(jnp.minimum(x, EXP_CLAMP))


def _hgrn_direction(q_ref, k_ref, lf_ref, v_ref, o_ref, state_ref, slot, reverse):
    n = HGRN_CHUNK
    rr = lax.broadcasted_iota(jnp.int32, (n, n), 0)
    cc = lax.broadcasted_iota(jnp.int32, (n, n), 1)
    causal = (cc >= rr) if reverse else (cc <= rr)
    g_all = _cumsum_rows(jnp.where(causal, 1.0, 0.0), lf_ref[0])
    last_row = 0 if reverse else n - 1
    mid_row = n // 2 if reverse else n // 2 - 1
    for hd in range(HGRN_HEADS):
        ln = slice(hd * HGRN_DK, (hd + 1) * HGRN_DK)
        g = g_all[:, ln]
        q = q_ref[0, :, ln].astype(F32)
        k = k_ref[0, :, ln].astype(F32)
        v = v_ref[0, :, ln]
        g_tot = g[last_row:last_row + 1, :]
        g_mid = g[mid_row:mid_row + 1, :]
        st = state_ref[slot, hd]
        o_inter = _dot_nt((q * jnp.exp(g)).astype(BF16), st.astype(BF16))
        qa = (q * _safe_exp(g - g_mid)).astype(BF16)
        kb = (k * _safe_exp(g_mid - g)).astype(BF16)
        a = jnp.where(causal, _dot_nt(qa, kb), 0.0)
        o_ref[0, :, ln] = o_inter + _dot(a.astype(BF16), v)
        kd = (k * jnp.exp(g_tot - g)).astype(BF16)
        state_ref[slot, hd] = st * jnp.exp(g_tot) + _dot_tn(v, kd)


def _hgrn_kernel(qf_ref, kf_ref, lf_ref, vf_ref, qb_ref, kb_ref, lb_ref, vb_ref, of_ref, ob_ref, state_ref):
    @pl.when(pl.program_id(1) == 0)
    def _():
        state_ref[...] = jnp.zeros(state_ref.shape, F32)

    _hgrn_direction(qf_ref, kf_ref, lf_ref, vf_ref, of_ref, state_ref, 0, False)
    _hgrn_direction(qb_ref, kb_ref, lb_ref, vb_ref, ob_ref, state_ref, 1, True)


def _hgrn(hq, hk, hlf, hv):
    bsz = hq.shape[0]
    lat = HGRN_CHUNKS - HGRN_CTX_CHUNKS

    def fwd(c):
        return jnp.where(c < HGRN_CTX_CHUNKS, lat + c, c - HGRN_CTX_CHUNKS)

    def bwd(c):
        return HGRN_CHUNKS - 1 - c

    def spec(order, lane_block):
        return pl.BlockSpec((1, HGRN_CHUNK, BRANCH_WIDTH), lambda b, c: (b, order(c), lane_block))

    out = jax.ShapeDtypeStruct((bsz, TOKENS, BRANCH_WIDTH), F32)
    return pl.pallas_call(
        _hgrn_kernel,
        out_shape=(out, out),
        grid=(bsz, HGRN_CHUNKS),
        in_specs=[spec(fwd, 0), spec(fwd, 0), spec(fwd, 0), spec(fwd, 0),
                  spec(bwd, 0), spec(bwd, 1), spec(bwd, 1), spec(bwd, 0)],
        out_specs=(spec(fwd, 0), spec(bwd, 0)),
        scratch_shapes=[pltpu.VMEM((2, HGRN_HEADS, HGRN_DK, HGRN_DK), F32)],
        compiler_params=_params(("parallel", "arbitrary")),
        name="hgrn",
    )(hq, hk, hlf, hv, hq, hk, hlf, hv)


def _fourier_kernel(xl_ref, xc_ref, wc_ref, csl_ref, csc_ref, o_ref, uv_ref):
    t = pl.program_id(1)

    @pl.when(t == 0)
    def _():
        uv = _dot(xl_ref[0], wc_ref[...])
        uv_ref[0:SEQ, :] = uv[:, :BRANCH_WIDTH].astype(BF16)
        uv_ref[SEQ:2 * SEQ, :] = uv[:, BRANCH_WIDTH:].astype(BF16)

    @pl.when(t < LATENT_ROW_TILES)
    def _():
        r0 = pl.multiple_of(t * ROW_TILE, ROW_TILE)
        y = _dot(csl_ref[pl.ds(r0, ROW_TILE), :], uv_ref[...])
        o_ref[0] = (y * (1.0 / math.sqrt(SEQ * FOURIER_GROUP_DIM))).astype(BF16)

    @pl.when(t == LATENT_ROW_TILES)
    def _():
        uv = _dot(xc_ref[0], wc_ref[...])
        uvc = jnp.concatenate([uv[:, :BRANCH_WIDTH], uv[:, BRANCH_WIDTH:]], axis=0).astype(BF16)
        y = _dot(csc_ref[...], uvc)
        o_ref[0] = (y * (1.0 / math.sqrt(CTX_LEN * FOURIER_GROUP_DIM))).astype(BF16)


def _fourier(four, wc, cs_lat, cs_ctx):
    bsz = four.shape[0]
    return pl.pallas_call(
        _fourier_kernel,
        out_shape=jax.ShapeDtypeStruct((bsz, TOKENS, BRANCH_WIDTH), BF16),
        grid=(bsz, ROW_TILES),
        in_specs=[pl.BlockSpec((1, SEQ, BRANCH_WIDTH), lambda b, t: (b, 0, 0)),
                  pl.BlockSpec((1, CTX_LEN, BRANCH_WIDTH), lambda b, t: (b, SEQ // CTX_LEN, 0)),
                  _const_spec(wc.shape), _const_spec(cs_lat.shape), _const_spec(cs_ctx.shape)],
        out_specs=pl.BlockSpec((1, ROW_TILE, BRANCH_WIDTH), lambda b, t: (b, t, 0)),
        scratch_shapes=[pltpu.VMEM((2 * SEQ, BRANCH_WIDTH), BF16)],
        compiler_params=_params(("parallel", "arbitrary")),
        name="fourier",
    )(four, four, wc, cs_lat, cs_ctx)


def _dft_tables(n):
    t = jnp.arange(n, dtype=jnp.int32)
    ang = ((t[:, None] * t[None, :]) % n).astype(F32) * (2.0 * math.pi / n)
    return jnp.concatenate([jnp.cos(ang), -jnp.sin(ang)], axis=1).astype(BF16)


def _channel_dft_table():
    c = jnp.arange(BRANCH_WIDTH, dtype=jnp.int32)
    same = (c[:, None] // FOURIER_GROUP_DIM) == (c[None, :] // FOURIER_GROUP_DIM)
    ang = (((c[:, None] % FOURIER_GROUP_DIM) * (c[None, :] % FOURIER_GROUP_DIM)) % FOURIER_GROUP_DIM
           ).astype(F32) * (2.0 * math.pi / FOURIER_GROUP_DIM)
    cos = jnp.where(same, jnp.cos(ang), 0.0)
    sin = jnp.where(same, jnp.sin(ang), 0.0)
    return jnp.concatenate([cos, sin], axis=1).astype(BF16)


def _rms(x):
    return x * lax.rsqrt(jnp.mean(x * x, axis=-1, keepdims=True) + EPS)


def _merge_mlp_kernel(x_ref, mod_ref, of_ref, oa_ref, hf_ref, hb_ref, hg_ref, gate_ref, hng_ref, g2_ref,
                      wb_ref, wo_ref, w1_ref, w2_ref, out_ref):
    d = D_MODEL
    x = x_ref[0]
    mod = mod_ref[...]

    o_h = hf_ref[0] + hb_ref[0]
    o_n = jnp.concatenate(
        [_rms(o_h[:, hd * HGRN_DK:(hd + 1) * HGRN_DK]) * hng_ref[...] for hd in range(HGRN_HEADS)], axis=-1)
    o_hr = (o_n * _silu(hg_ref[0].astype(F32))).astype(BF16)

    branches = (of_ref[0], oa_ref[0], o_hr)
    mix = None
    for n, br in enumerate(branches):
        term = _sigmoid(gate_ref[0, :, n * d:(n + 1) * d].astype(F32)) * _dot(br, wb_ref[n])
        mix = term if mix is None else mix + term
    y = _dot(mix.astype(BF16), wo_ref[...])
    x1 = x + mod[:, 2 * d:3 * d] * y

    h2 = (_rms(x1) * g2_ref[...] * (1.0 + mod[:, 4 * d:5 * d]) + mod[:, 3 * d:4 * d]).astype(BF16)
    u = jnp.square(jnp.maximum(_dot(h2, w1_ref[...]), 0.0)).astype(BF16)
    out_ref[0] = x1 + mod[:, 5 * d:6 * d] * _dot(u, w2_ref[...])


def _merge_mlp(xs, mods_l, o_four, o_attn, o_hf, o_hb, hg, gate, hng, g2, wb, wo, w1, w2):
    bsz = xs.shape[0]
    row = lambda w: pl.BlockSpec((1, ROW_TILE, w), lambda b, t: (b, t, 0))
    return pl.pallas_call(
        _merge_mlp_kernel,
        out_shape=jax.ShapeDtypeStruct(xs.shape, F32),
        grid=(bsz, ROW_TILES),
        in_specs=[row(D_MODEL),
                  pl.BlockSpec((None, 1, 6 * D_MODEL), lambda b, t: (_mod_row_index(b, t), 0, 0)),
                  row(BRANCH_WIDTH), row(BRANCH_WIDTH), row(BRANCH_WIDTH), row(BRANCH_WIDTH),
                  row(BRANCH_WIDTH), row(N_BRANCH * D_MODEL),
                  _const_spec((1, HGRN_DK)), _const_spec((1, D_MODEL)),
                  _const_spec(wb.shape), _const_spec(wo.shape), _const_spec(w1.shape), _const_spec(w2.shape)],
        out_specs=row(D_MODEL),
        compiler_params=_params(("parallel", "parallel")),
        name="merge_mlp",
    )(xs, mods_l, o_four, o_attn, o_hf, o_hb, hg, gate, hng, g2, wb, wo, w1, w2)


def _rope_tables():
    pos = jnp.arange(SEQ)
    row = (pos // GRID_W).astype(F32)
    col = (pos % GRID_W).astype(F32)
    axis_dim = HEAD_DIM // 2
    half = axis_dim // 2
    inv_freq = ROPE_THETA ** (-jnp.arange(0, axis_dim, 2, dtype=F32) / axis_dim)
    ang_r = row[:, None] * inv_freq
    ang_c = col[:, None] * inv_freq
    ang = jnp.concatenate([ang_r, ang_r, ang_c, ang_c], axis=1)
    sign = jnp.tile(jnp.concatenate([-jnp.ones(half, F32), jnp.ones(half, F32)]), 2)
    cos = jnp.concatenate([jnp.cos(ang), jnp.ones((CTX_LEN, HEAD_DIM), F32)], axis=0)
    sin = jnp.concatenate([jnp.sin(ang) * sign, jnp.zeros((CTX_LEN, HEAD_DIM), F32)], axis=0)
    return jnp.tile(cos, (1, 2)), jnp.tile(sin, (1, 2))


def kernel(x, c, ctx, c_ctx, w_mod, b_mod, norm1_g, norm2_g, w_in, q_norm_g, k_norm_g, attn_sink,
           hgrn_lb_logits, hgrn_norm_g, w_branch, w_out, w_ff1, w_ff2):
    bsz, n_tok, d = x.shape
    depth = w_mod.shape[0]
    assert (n_tok, d, ctx.shape[1]) == (SEQ, D_MODEL, CTX_LEN) and bsz <= MOD_ROWS // 2

    xs = jnp.concatenate([x, ctx], axis=1)
    cond = jnp.zeros((MOD_ROWS, d), F32).at[:bsz].set(c).at[MOD_ROWS // 2].set(c_ctx)
    mods = _modulation(cond, w_mod, b_mod).reshape(depth, MOD_ROWS, 1, 6 * d)

    cos, sin = _rope_tables()
    wc = _channel_dft_table()
    cs_lat = _dft_tables(SEQ)
    cs_ctx = _dft_tables(CTX_LEN)
    lb_logits = hgrn_lb_logits.reshape(depth * 2, BRANCH_WIDTH)

    for l in range(depth):
        q, k_dup, v_dup, hv, hk, hlf, hq, hg, four, gate = _inproj(
            xs, mods[l], norm1_g[l].reshape(1, d), w_in[l].astype(BF16),
            jnp.tile(q_norm_g[l], 2).reshape(1, LANES), jnp.tile(k_norm_g[l], 2).reshape(1, LANES),
            cos, sin, lb_logits, layer=l)
        o_attn = _attention(q, k_dup, v_dup, attn_sink[l])
        o_hf, o_hb = _hgrn(hq, hk, hlf, hv)
        o_four = _fourier(four, wc, cs_lat, cs_ctx)
        xs = _merge_mlp(xs, mods[l], o_four, o_attn, o_hf, o_hb, hg, gate,
                        hgrn_norm_g[l].reshape(1, HGRN_DK), norm2_g[l].reshape(1, d),
                        w_branch[l].astype(BF16), w_out[l].astype(BF16),
                        w_ff1[l].astype(BF16), w_ff2[l].astype(BF16))
    return xs[:, :SEQ]
```

```python
import functools
import math

import jax
import jax.numpy as jnp
from jax import lax
from jax.experimental import pallas as pl
from jax.experimental.pallas import tpu as pltpu

F32 = jnp.float32
BF16 = jnp.bfloat16

D_MODEL = 1024
SEQ = 2048
CTX_LEN = 256
TOKENS = SEQ + CTX_LEN
GRID_W = 64
EPS = 1e-6
NEG_BIG = -1e30
LB_FLOOR = 1e-30
ROPE_THETA = 10000.0

BRANCH_WIDTH = D_MODEL // 2
HEAD_DIM = 64
ATTN_HEADS = BRANCH_WIDTH // HEAD_DIM
ATTN_KV_HEADS = ATTN_HEADS // 4
KV_WIDTH = ATTN_KV_HEADS * HEAD_DIM
ATTN_BLOCK = 128
FOURIER_GROUPS = 4
FOURIER_GROUP_DIM = BRANCH_WIDTH // FOURIER_GROUPS
HGRN_HEADS = 4
HGRN_DK = BRANCH_WIDTH // HGRN_HEADS
D_FF = 4 * D_MODEL
N_BRANCH = 3

_SEG = (KV_WIDTH, KV_WIDTH, BRANCH_WIDTH, BRANCH_WIDTH, BRANCH_WIDTH,
        BRANCH_WIDTH, BRANCH_WIDTH, BRANCH_WIDTH, BRANCH_WIDTH, N_BRANCH * D_MODEL)
_OFF = [sum(_SEG[:i]) for i in range(len(_SEG) + 1)]
D_IN = _OFF[-1]

LANES = 128
VMEM_LIMIT_BYTES = 56 * 1024 * 1024

ROW_TILE = 256
ROW_TILES = TOKENS // ROW_TILE
LATENT_ROW_TILES = SEQ // ROW_TILE
HGRN_CHUNK = 64
HGRN_CHUNKS = TOKENS // HGRN_CHUNK
HGRN_CTX_CHUNKS = CTX_LEN // HGRN_CHUNK
HGRN_BATCH = 8
EXP2_CLAMP = 100.0
LOG2E = 1.0 / math.log(2.0)
MOD_ROWS = 16


def _const_spec(shape):
    nd = len(shape)
    return pl.BlockSpec(shape, lambda *_: (0,) * nd, pipeline_mode=pl.Buffered(1))


def _params(semantics, flags=None):
    return pltpu.CompilerParams(dimension_semantics=semantics, vmem_limit_bytes=VMEM_LIMIT_BYTES,
                                flags=flags)


def _dot(a, b):
    return jnp.dot(a, b, preferred_element_type=F32)


def _dot_nt(a, b):
    return lax.dot_general(a, b, (((1,), (1,)), ((), ())), preferred_element_type=F32)


def _dot_tn(a, b):
    return lax.dot_general(a, b, (((0,), (0,)), ((), ())), preferred_element_type=F32)


def _sigmoid(x):
    return 1.0 / (1.0 + jnp.exp(-x))


def _silu(x):
    return x * _sigmoid(x)


def _mod_row_index(b, t):
    return jnp.where(t < LATENT_ROW_TILES, b, MOD_ROWS // 2)


MOD_COL_TILE = 1536


def _modulation_kernel(cond_ref, w_ref, b_ref, out_ref):
    a = _silu(cond_ref[...]).astype(BF16)
    out_ref[...] = _dot(a, w_ref[...].astype(BF16)) + b_ref[...]


def _modulation(cond, w_mod, b_mod):
    depth = w_mod.shape[0]
    n = w_mod.shape[2]
    return pl.pallas_call(
        _modulation_kernel,
        out_shape=jax.ShapeDtypeStruct((depth, MOD_ROWS, n), F32),
        grid=(depth, n // MOD_COL_TILE),
        in_specs=[
            pl.BlockSpec((MOD_ROWS, D_MODEL), lambda l, j: (0, 0)),
            pl.BlockSpec((None, D_MODEL, MOD_COL_TILE), lambda l, j: (l, 0, j)),
            pl.BlockSpec((None, 1, MOD_COL_TILE), lambda l, j: (l, 0, j)),
        ],
        out_specs=pl.BlockSpec((None, MOD_ROWS, MOD_COL_TILE), lambda l, j: (l, 0, j)),
        compiler_params=_params(("parallel", "parallel")),
        name="modulation",
    )(cond, w_mod, b_mod.reshape(depth, 1, n))


def _lane_index(shape):
    return lax.broadcasted_iota(jnp.int32, shape, len(shape) - 1)


def _head_mean_matrix():
    shift = HEAD_DIM.bit_length() - 1
    r = lax.broadcasted_iota(jnp.int32, (LANES, LANES), 0) >> shift
    c = lax.broadcasted_iota(jnp.int32, (LANES, LANES), 1) >> shift
    return jnp.where(r == c, 1.0 / HEAD_DIM, 0.0).astype(BF16)


def _head_rms_norm(u, gain, mean_bd):
    ms = _dot((u * u).astype(BF16), mean_bd)
    return u * lax.rsqrt(ms + EPS) * gain


def _rope(u, cos, sin_signed):
    first = (_lane_index(u.shape) & 31) < 16
    partner = jnp.where(first, pltpu.roll(u, LANES - 16, axis=1), pltpu.roll(u, 16, axis=1))
    return u * cos + partner * sin_signed


def _hgrn_lower_bound(lb_ref, layer, direction):
    depth = lb_ref.shape[0] // 2
    rows = [lb_ref[2 * j + direction:2 * j + direction + 1, :] for j in range(depth)]
    m = functools.reduce(jnp.maximum, rows)
    e = [jnp.exp(r - m) for r in rows]
    z = functools.reduce(lambda a, b: a + b, e)
    p = [x / z for x in e]
    csum = functools.reduce(lambda a, b: a + b, p[:layer + 1])
    return csum - p[0]


def _hgrn_forget(z, lb):
    t = jnp.exp(-jnp.abs(z))
    pos = z >= 0.0
    lb_floor = jnp.maximum(lb, LB_FLOOR)
    r = 1.0 / (1.0 + t)
    k = (1.0 - lb) * jnp.where(pos, t * r, r)
    f = jnp.where(pos, 1.0 + lb_floor * t, t + lb_floor) * r
    return k, jnp.log(f) * LOG2E


GATE_CHUNKS = 6


def _inproj_kernel(x_ref, mod_ref, g1_ref, w_ref, qg_ref, kg_ref, cos_ref, sin_ref, lb_ref,
                   q_ref, k_ref, v_ref, hv_ref, hk_ref, hlf_ref, hq_ref, hg_ref, four_ref, gate_ref,
                   *, layer):
    x = x_ref[0]
    ms = jnp.mean(x * x, axis=-1, keepdims=True)
    row_gain = g1_ref[...] * (1.0 + mod_ref[:, D_MODEL:2 * D_MODEL])
    h = (x * lax.rsqrt(ms + EPS) * row_gain + mod_ref[:, 0:D_MODEL]).astype(BF16)

    def proj(seg, width=None):
        a = _OFF[seg]
        b = _OFF[seg + 1] if width is None else a + width
        return _dot(h, w_ref[:, a:b])

    gate_width = N_BRANCH * D_MODEL // GATE_CHUNKS

    def gate_chunk(i):
        a = _OFF[9] + i * gate_width
        gate_ref[0, :, i * gate_width:(i + 1) * gate_width] = _dot(h, w_ref[:, a:a + gate_width]).astype(BF16)

    cos = cos_ref[...]
    sin = sin_ref[...]
    low_head = _lane_index((ROW_TILE, LANES)) < HEAD_DIM
    mean_bd = _head_mean_matrix()

    kv = proj(0, 2 * KV_WIDTH)
    gate_chunk(0)
    kr = _rope(_head_rms_norm(kv[:, :KV_WIDTH], kg_ref[...], mean_bd), cos, sin)
    kr_sw = pltpu.roll(kr, HEAD_DIM, axis=1)
    k_ref[0, 0] = jnp.where(low_head, kr, kr_sw).astype(BF16)
    k_ref[0, 1] = jnp.where(low_head, kr_sw, kr).astype(BF16)
    vr = kv[:, KV_WIDTH:]
    vr_sw = pltpu.roll(vr, HEAD_DIM, axis=1)
    v_ref[0, 0] = jnp.where(low_head, vr, vr_sw).astype(BF16)
    v_ref[0, 1] = jnp.where(low_head, vr_sw, vr).astype(BF16)

    qr = proj(5)
    gate_chunk(1)
    q_gain = qg_ref[...] * (HEAD_DIM ** -0.5 * LOG2E)
    for c in range(BRANCH_WIDTH // LANES):
        u = qr[:, c * LANES:(c + 1) * LANES]
        q_ref[0, :, c * LANES:(c + 1) * LANES] = _rope(_head_rms_norm(u, q_gain, mean_bd), cos, sin).astype(BF16)

    for direction, seg in ((0, 3), (1, 4)):
        z = proj(seg)
        gate_chunk(2 + direction)
        k, log2f = _hgrn_forget(z, _hgrn_lower_bound(lb_ref, layer, direction))
        hk_ref[0, :, direction * BRANCH_WIDTH:(direction + 1) * BRANCH_WIDTH] = k.astype(BF16)
        hlf_ref[0, :, direction * BRANCH_WIDTH:(direction + 1) * BRANCH_WIDTH] = log2f
    qh = proj(6)
    gate_chunk(4)
    hq_ref[0] = _silu(qh).astype(BF16)
    hv_ref[0] = proj(2).astype(BF16)
    hg_ref[0] = proj(7).astype(BF16)
    four_ref[0] = proj(8).astype(BF16)
    gate_chunk(5)


def _inproj(xs, mods_l, g1, w_in_l, qg, kg, cos, sin, lb_logits, *, layer):
    bsz = xs.shape[0]
    row = lambda w: pl.BlockSpec((1, ROW_TILE, w), lambda b, t: (b, t, 0))
    dup = pl.BlockSpec((1, ATTN_KV_HEADS, ROW_TILE, LANES), lambda b, t: (b, 0, t, 0))
    tok = lambda w, dt: jax.ShapeDtypeStruct((bsz, TOKENS, w), dt)
    dup_shape = jax.ShapeDtypeStruct((bsz, ATTN_KV_HEADS, TOKENS, LANES), BF16)
    return pl.pallas_call(
        functools.partial(_inproj_kernel, layer=layer),
        out_shape=(tok(BRANCH_WIDTH, BF16), dup_shape, dup_shape,
                   tok(BRANCH_WIDTH, BF16), tok(2 * BRANCH_WIDTH, BF16), tok(2 * BRANCH_WIDTH, F32),
                   tok(BRANCH_WIDTH, BF16), tok(BRANCH_WIDTH, BF16), tok(BRANCH_WIDTH, BF16),
                   tok(N_BRANCH * D_MODEL, BF16)),
        grid=(bsz, ROW_TILES),
        in_specs=[
            row(D_MODEL),
            pl.BlockSpec((None, 1, 6 * D_MODEL), lambda b, t: (_mod_row_index(b, t), 0, 0)),
            _const_spec((1, D_MODEL)),
            _const_spec((D_MODEL, D_IN)),
            _const_spec((1, LANES)),
            _const_spec((1, LANES)),
            pl.BlockSpec((ROW_TILE, LANES), lambda b, t: (t, 0)),
            pl.BlockSpec((ROW_TILE, LANES), lambda b, t: (t, 0)),
            _const_spec(lb_logits.shape),
        ],
        out_specs=(row(BRANCH_WIDTH), dup, dup,
                   row(BRANCH_WIDTH), row(2 * BRANCH_WIDTH), row(2 * BRANCH_WIDTH),
                   row(BRANCH_WIDTH), row(BRANCH_WIDTH), row(BRANCH_WIDTH),
                   row(N_BRANCH * D_MODEL)),
        compiler_params=_params(("parallel", "parallel")),
        name="inproj",
    )(xs, mods_l, g1, w_in_l, qg, kg, cos, sin, lb_logits)


ATTN_BLOCKS = TOKENS // ATTN_BLOCK
LATENT_ATTN_BLOCKS = SEQ // ATTN_BLOCK
GROUP = ATTN_HEADS // ATTN_KV_HEADS
ATTN_ROWS = GROUP * ATTN_BLOCK


def _attend_heads(sink_ref, q_ref, o_ref, kv_blocks, masks):
    low_head = _lane_index((ATTN_BLOCK, LANES)) < HEAD_DIM
    zero = jnp.zeros((), BF16)
    kv = [kv_blocks(hk) for hk in range(ATTN_KV_HEADS)]
    heads = [dict(head=h) for h in range(ATTN_HEADS)]

    def scores_stage(st):
        h = st["head"]
        tile = q_ref[0, :, (h // 2) * LANES:(h // 2 + 1) * LANES]
        qh = jnp.where(low_head, zero, tile) if h % 2 else jnp.where(low_head, tile, zero)
        scores = [_dot_nt(qh, kb) for kb in kv[h // GROUP][0]]
        st["scores"] = [s if mk is None else jnp.where(mk, s, NEG_BIG) for s, mk in zip(scores, masks)]

    def max_stage(st):
        st["sink"] = sink_ref[st["head"]] * LOG2E
        m = jnp.full((ATTN_BLOCK, 1), st["sink"], F32)
        for s in st["scores"]:
            m = jnp.maximum(m, jnp.max(s, axis=-1, keepdims=True))
        st["m"] = m

    def values_stage(st):
        denom = jnp.exp2(st["sink"] - st["m"])
        o = None
        for s, vb in zip(st["scores"], kv[st["head"] // GROUP][1]):
            p = jnp.exp2(s - st["m"])
            denom = denom + jnp.sum(p, axis=-1, keepdims=True)
            pv = _dot(p.astype(BF16), vb)
            o = pv if o is None else o + pv
        st["o"] = o / denom
        h = st["head"]
        if h % 2:
            o_ref[0, :, (h // 2) * LANES:(h // 2 + 1) * LANES] = jnp.where(
                low_head, heads[h - 1]["o"], st["o"]).astype(BF16)

    stages = (scores_stage, max_stage, values_stage)
    for tick in range(ATTN_HEADS + len(stages) - 1):
        for depth, stage in enumerate(stages):
            if 0 <= tick - depth < ATTN_HEADS:
                stage(heads[tick - depth])


def _attention_kernel(sink_ref, q_ref, k_ref, v_ref, o_ref):
    i = pl.program_id(1)
    prev_start = pl.multiple_of(jnp.maximum(i - 1, 0) * ATTN_BLOCK, ATTN_BLOCK)
    cur_start = pl.multiple_of(i * ATTN_BLOCK, ATTN_BLOCK)
    next_start = pl.multiple_of(jnp.minimum(i + 1, LATENT_ATTN_BLOCKS - 1) * ATTN_BLOCK, ATTN_BLOCK)

    def blocks(ref, hk):
        near = jnp.concatenate([ref[0, hk, pl.ds(prev_start, ATTN_BLOCK), :],
                                ref[0, hk, pl.ds(cur_start, ATTN_BLOCK), :]], axis=0)
        return near, ref[0, hk, pl.ds(next_start, ATTN_BLOCK), :], ref[0, hk, SEQ:TOKENS, :]

    t2 = lax.broadcasted_iota(jnp.int32, (ATTN_BLOCK, 2 * ATTN_BLOCK), 0)
    j2 = lax.broadcasted_iota(jnp.int32, (ATTN_BLOCK, 2 * ATTN_BLOCK), 1)
    near_valid = jnp.logical_or(j2 >= ATTN_BLOCK, jnp.logical_and(j2 >= t2, i > 0))
    t1 = lax.broadcasted_iota(jnp.int32, (ATTN_BLOCK, ATTN_BLOCK), 0)
    j1 = lax.broadcasted_iota(jnp.int32, (ATTN_BLOCK, ATTN_BLOCK), 1)
    next_valid = jnp.logical_and(j1 <= t1, i < LATENT_ATTN_BLOCKS - 1)
    _attend_heads(sink_ref, q_ref, o_ref, lambda hk: (blocks(k_ref, hk), blocks(v_ref, hk)),
                  (near_valid, next_valid, None))


def _context_attention_kernel(sink_ref, q_ref, kx_ref, vx_ref, o_in_ref, o_ref):
    del o_in_ref
    _attend_heads(sink_ref, q_ref, o_ref, lambda hk: ((kx_ref[0, hk],), (vx_ref[0, hk],)), (None,))


def _attention(q, k_dup, v_dup, sink):
    bsz = q.shape[0]
    stream = pl.BlockSpec((1, ATTN_KV_HEADS, TOKENS, LANES), lambda b, i: (b, 0, 0, 0))
    ctx = pl.BlockSpec((1, ATTN_KV_HEADS, CTX_LEN, LANES), lambda b, i: (b, 0, SEQ // CTX_LEN, 0))
    smem = pl.BlockSpec(memory_space=pltpu.SMEM)
    out_shape = jax.ShapeDtypeStruct((bsz, TOKENS, BRANCH_WIDTH), BF16)
    o_lat = pl.pallas_call(
        _attention_kernel,
        out_shape=out_shape,
        grid=(bsz, LATENT_ATTN_BLOCKS),
        in_specs=[smem, pl.BlockSpec((1, ATTN_BLOCK, BRANCH_WIDTH), lambda b, i: (b, i, 0)), stream, stream],
        out_specs=pl.BlockSpec((1, ATTN_BLOCK, BRANCH_WIDTH), lambda b, i: (b, i, 0)),
        compiler_params=_params(("parallel", "arbitrary")),
        name="attention",
    )(sink, q, k_dup, v_dup)
    ctx_rows = pl.BlockSpec((1, ATTN_BLOCK, BRANCH_WIDTH), lambda b, i: (b, LATENT_ATTN_BLOCKS + i, 0))
    return pl.pallas_call(
        _context_attention_kernel,
        out_shape=out_shape,
        grid=(bsz, ATTN_BLOCKS - LATENT_ATTN_BLOCKS),
        in_specs=[smem, ctx_rows, ctx, ctx, pl.BlockSpec(memory_space=pl.ANY)],
        out_specs=ctx_rows,
        input_output_aliases={4: 0},
        compiler_params=_params(("parallel", "parallel")),
        name="context_attention",
    )(sink, q, k_dup, v_dup, o_lat)


def _cumsum_rows(tri, x):
    hi = x.astype(BF16)
    lo = (x - hi.astype(F32)).astype(BF16)
    return _dot(tri, hi) + _dot(tri, lo)


def _hgrn_kernel(qf_ref, kf_ref, lf_ref, vf_ref, qb_ref, kb_ref, lb_ref, vb_ref, of_ref, ob_ref, state_ref):
    @pl.when(pl.program_id(1) == 0)
    def _():
        state_ref[...] = jnp.zeros(state_ref.shape, F32)

    n = HGRN_CHUNK
    rr = lax.broadcasted_iota(jnp.int32, (n, n), 0)
    cc = lax.broadcasted_iota(jnp.int32, (n, n), 1)
    directions = ((0, cc <= rr, n - 1, n // 2 - 1, qf_ref, kf_ref, lf_ref, vf_ref, of_ref),
                  (1, cc >= rr, 0, n // 2, qb_ref, kb_ref, lb_ref, vb_ref, ob_ref))

    chains = []
    for slot, causal, last_row, mid_row, q_ref, k_ref, lf_ref_d, v_ref, o_ref in directions:
        tri = jnp.where(causal, 1.0, 0.0).astype(BF16)
        for bi in range(HGRN_BATCH):
            g_all = _cumsum_rows(tri, lf_ref_d[bi])
            for hd in range(HGRN_HEADS):
                ln = slice(hd * HGRN_DK, (hd + 1) * HGRN_DK)
                g = g_all[:, ln]
                chains.append(dict(
                    causal=causal, g=g, g_tot=g[last_row:last_row + 1, :], g_mid=g[mid_row:mid_row + 1, :],
                    q=q_ref[bi, :, ln], k=k_ref[bi, :, ln], v=v_ref[bi, :, ln],
                    o_ref=o_ref, state=(bi, slot, hd), out=(bi, slice(None), ln)))

    for ch in chains:
        d = ch["g"] - ch["g_mid"]
        ch["qa"] = ch["q"] * jnp.exp2(jnp.minimum(d, EXP2_CLAMP)).astype(BF16)
        ch["kb"] = ch["k"] * jnp.exp2(jnp.minimum(-d, EXP2_CLAMP)).astype(BF16)
        ch["st_mid"] = state_ref[ch["state"]] * jnp.exp2(ch["g_mid"])
    for ch in chains:
        ch["a"] = _dot_nt(ch["qa"], ch["kb"])
        ch["o_inter"] = _dot_nt(ch["qa"], ch["st_mid"].astype(BF16))
        ch["u"] = _dot_tn(ch["v"], ch["kb"])
    for ch in chains:
        a = jnp.where(ch["causal"], ch["a"], 0.0).astype(BF16)
        ch["o_ref"][ch["out"]] = (ch["o_inter"] + _dot(a, ch["v"])).astype(ch["o_ref"].dtype)
        state_ref[ch["state"]] = (ch["st_mid"] + ch["u"]) * jnp.exp2(ch["g_tot"] - ch["g_mid"])


def _hgrn(hq, hk, hlf, hv):
    bsz = hq.shape[0]
    assert bsz % HGRN_BATCH == 0
    lat = HGRN_CHUNKS - HGRN_CTX_CHUNKS

    def fwd(c):
        return jnp.where(c < HGRN_CTX_CHUNKS, lat + c, c - HGRN_CTX_CHUNKS)

    def bwd(c):
        return HGRN_CHUNKS - 1 - c

    def spec(order, lane_block):
        return pl.BlockSpec((HGRN_BATCH, HGRN_CHUNK, BRANCH_WIDTH), lambda b, c: (b, order(c), lane_block))

    out = jax.ShapeDtypeStruct((bsz, TOKENS, BRANCH_WIDTH), BF16)
    return pl.pallas_call(
        _hgrn_kernel,
        out_shape=(out, out),
        grid=(bsz // HGRN_BATCH, HGRN_CHUNKS),
        in_specs=[spec(fwd, 0), spec(fwd, 0), spec(fwd, 0), spec(fwd, 0),
                  spec(bwd, 0), spec(bwd, 1), spec(bwd, 1), spec(bwd, 0)],
        out_specs=(spec(fwd, 0), spec(bwd, 0)),
        scratch_shapes=[pltpu.VMEM((HGRN_BATCH, 2, HGRN_HEADS, HGRN_DK, HGRN_DK), F32)],
        compiler_params=_params(("parallel", "arbitrary")),
        name="hgrn",
    )(hq, hk, hlf, hv, hq, hk, hlf, hv)


FOURIER_ROW_TILE = 1024


def _fourier_kernel(x_ref, wc_ref, cs_ref, o_ref, uv_ref):
    t = pl.program_id(1)

    @pl.when(t == 0)
    def _():
        uv = _dot(x_ref[0], wc_ref[...])
        uv_ref[0:SEQ, :] = uv[:, :BRANCH_WIDTH].astype(BF16)
        uv_ref[SEQ:2 * SEQ, :] = uv[:, BRANCH_WIDTH:].astype(BF16)

    r0 = pl.multiple_of(t * FOURIER_ROW_TILE, FOURIER_ROW_TILE)
    y = _dot(cs_ref[pl.ds(r0, FOURIER_ROW_TILE), :], uv_ref[...])
    o_ref[0] = (y * (1.0 / math.sqrt(SEQ * FOURIER_GROUP_DIM))).astype(BF16)


def _context_fourier_kernel(x_ref, wc_ref, cs_ref, o_in_ref, o_ref):
    del o_in_ref
    uv = _dot(x_ref[0], wc_ref[...])
    uvc = jnp.concatenate([uv[:, :BRANCH_WIDTH], uv[:, BRANCH_WIDTH:]], axis=0).astype(BF16)
    y = _dot(cs_ref[...], uvc)
    o_ref[0] = (y * (1.0 / math.sqrt(CTX_LEN * FOURIER_GROUP_DIM))).astype(BF16)


def _fourier(four, wc, cs_lat, cs_ctx):
    bsz = four.shape[0]
    out_shape = jax.ShapeDtypeStruct((bsz, TOKENS, BRANCH_WIDTH), BF16)
    o_lat = pl.pallas_call(
        _fourier_kernel,
        out_shape=out_shape,
        grid=(bsz, SEQ // FOURIER_ROW_TILE),
        in_specs=[pl.BlockSpec((1, SEQ, BRANCH_WIDTH), lambda b, t: (b, 0, 0)),
                  _const_spec(wc.shape), _const_spec(cs_lat.shape)],
        out_specs=pl.BlockSpec((1, FOURIER_ROW_TILE, BRANCH_WIDTH), lambda b, t: (b, t, 0)),
        scratch_shapes=[pltpu.VMEM((2 * SEQ, BRANCH_WIDTH), BF16)],
        compiler_params=_params(("parallel", "arbitrary")),
        name="fourier",
    )(four, wc, cs_lat)
    ctx_rows = pl.BlockSpec((1, CTX_LEN, BRANCH_WIDTH), lambda b: (b, SEQ // CTX_LEN, 0))
    return pl.pallas_call(
        _context_fourier_kernel,
        out_shape=out_shape,
        grid=(bsz,),
        in_specs=[ctx_rows, _const_spec(wc.shape), _const_spec(cs_ctx.shape),
                  pl.BlockSpec(memory_space=pl.ANY)],
        out_specs=ctx_rows,
        input_output_aliases={3: 0},
        compiler_params=_params(("parallel",)),
        name="context_fourier",
    )(four, wc, cs_ctx, o_lat)


def _dft_tables(n, split):
    col = jnp.arange(n, dtype=jnp.int32)

    def trig(row_factor):
        ang = ((row_factor[:, None] * col[None, :]) % n).astype(F32) * (2.0 * math.pi / n)
        return jnp.cos(ang), jnp.sin(ang)

    cos_a, sin_a = trig(split * jnp.arange(n // split, dtype=jnp.int32))
    cos_b, sin_b = trig(jnp.arange(split, dtype=jnp.int32))
    cos = cos_a[:, None, :] * cos_b[None, :, :] - sin_a[:, None, :] * sin_b[None, :, :]
    sin = sin_a[:, None, :] * cos_b[None, :, :] + cos_a[:, None, :] * sin_b[None, :, :]
    return jnp.concatenate([cos.reshape(n, n), -sin.reshape(n, n)], axis=1).astype(BF16)


def _channel_dft_table():
    c = jnp.arange(BRANCH_WIDTH, dtype=jnp.int32)
    same = (c[:, None] // FOURIER_GROUP_DIM) == (c[None, :] // FOURIER_GROUP_DIM)
    ang = (((c[:, None] % FOURIER_GROUP_DIM) * (c[None, :] % FOURIER_GROUP_DIM)) % FOURIER_GROUP_DIM
           ).astype(F32) * (2.0 * math.pi / FOURIER_GROUP_DIM)
    cos = jnp.where(same, jnp.cos(ang), 0.0)
    sin = jnp.where(same, jnp.sin(ang), 0.0)
    return jnp.concatenate([cos, sin], axis=1).astype(BF16)


def _rms(x):
    return x * lax.rsqrt(jnp.mean(x * x, axis=-1, keepdims=True) + EPS)


def _merge_mlp_kernel(x_ref, mod_ref, of_ref, oa_ref, hf_ref, hb_ref, hg_ref, gate_ref, hng_ref, g2_ref,
                      wb_ref, wo_ref, w1_ref, w2_ref, out_ref):
    d = D_MODEL
    x = x_ref[0]
    mod = mod_ref[...]

    o_h = hf_ref[0].astype(F32) + hb_ref[0].astype(F32)
    o_n = jnp.concatenate(
        [_rms(o_h[:, hd * HGRN_DK:(hd + 1) * HGRN_DK]) * hng_ref[...] for hd in range(HGRN_HEADS)], axis=-1)
    o_hr = (o_n * _silu(hg_ref[0].astype(F32))).astype(BF16)

    branches = (of_ref[0], oa_ref[0], o_hr)
    mix = None
    for n, br in enumerate(branches):
        term = _sigmoid(gate_ref[0, :, n * d:(n + 1) * d].astype(F32)) * _dot(br, wb_ref[n])
        mix = term if mix is None else mix + term
    y = _dot(mix.astype(BF16), wo_ref[...])
    x1 = x + mod[:, 2 * d:3 * d] * y

    h2 = (_rms(x1) * g2_ref[...] * (1.0 + mod[:, 4 * d:5 * d]) + mod[:, 3 * d:4 * d]).astype(BF16)
    u = jnp.square(jnp.maximum(_dot(h2, w1_ref[...]), 0.0)).astype(BF16)
    out_ref[0] = x1 + mod[:, 5 * d:6 * d] * _dot(u, w2_ref[...])


def _merge_mlp(xs, mods_l, o_four, o_attn, o_hf, o_hb, hg, gate, hng, g2, wb, wo, w1, w2, *, latent_only):
    bsz = xs.shape[0]
    row = lambda w: pl.BlockSpec((1, ROW_TILE, w), lambda b, t: (b, t, 0))
    tiles = LATENT_ROW_TILES if latent_only else ROW_TILES
    return pl.pallas_call(
        _merge_mlp_kernel,
        out_shape=jax.ShapeDtypeStruct((bsz, tiles * ROW_TILE, D_MODEL), F32),
        grid=(bsz, tiles),
        in_specs=[row(D_MODEL),
                  pl.BlockSpec((None, 1, 6 * D_MODEL), lambda b, t: (_mod_row_index(b, t), 0, 0)),
                  row(BRANCH_WIDTH), row(BRANCH_WIDTH), row(BRANCH_WIDTH), row(BRANCH_WIDTH),
                  row(BRANCH_WIDTH), row(N_BRANCH * D_MODEL),
                  _const_spec((1, HGRN_DK)), _const_spec((1, D_MODEL)),
                  _const_spec(wb.shape), _const_spec(wo.shape), _const_spec(w1.shape), _const_spec(w2.shape)],
        out_specs=row(D_MODEL),
        compiler_params=_params(("parallel", "parallel")),
        name="merge_mlp",
    )(xs, mods_l, o_four, o_attn, o_hf, o_hb, hg, gate, hng, g2, wb, wo, w1, w2)


def _rope_tables():
    pos = jnp.arange(SEQ)
    row = (pos // GRID_W).astype(F32)
    col = (pos % GRID_W).astype(F32)
    axis_dim = HEAD_DIM // 2
    half = axis_dim // 2
    inv_freq = ROPE_THETA ** (-jnp.arange(0, axis_dim, 2, dtype=F32) / axis_dim)
    ang_r = row[:, None] * inv_freq
    ang_c = col[:, None] * inv_freq
    ang = jnp.concatenate([ang_r, ang_r, ang_c, ang_c], axis=1)
    sign = jnp.tile(jnp.concatenate([-jnp.ones(half, F32), jnp.ones(half, F32)]), 2)
    cos = jnp.concatenate([jnp.cos(ang), jnp.ones((CTX_LEN, HEAD_DIM), F32)], axis=0)
    sin = jnp.concatenate([jnp.sin(ang) * sign, jnp.zeros((CTX_LEN, HEAD_DIM), F32)], axis=0)
    return jnp.tile(cos, (1, 2)), jnp.tile(sin, (1, 2))


def kernel(x, c, ctx, c_ctx, w_mod, b_mod, norm1_g, norm2_g, w_in, q_norm_g, k_norm_g, attn_sink,
           hgrn_lb_logits, hgrn_norm_g, w_branch, w_out, w_ff1, w_ff2):
    bsz, n_tok, d = x.shape
    depth = w_mod.shape[0]
    assert (n_tok, d, ctx.shape[1]) == (SEQ, D_MODEL, CTX_LEN) and bsz <= MOD_ROWS // 2

    xs = jnp.concatenate([x, ctx], axis=1)
    cond = jnp.zeros((MOD_ROWS, d), F32).at[:bsz].set(c).at[MOD_ROWS // 2].set(c_ctx)
    mods = _modulation(cond, w_mod, b_mod).reshape(depth, MOD_ROWS, 1, 6 * d)

    cos, sin = _rope_tables()
    wc = _channel_dft_table()
    cs_lat = _dft_tables(SEQ, 32)
    cs_ctx = _dft_tables(CTX_LEN, 16)
    lb_logits = hgrn_lb_logits.reshape(depth * 2, BRANCH_WIDTH)

    for l in range(depth):
        q, k_dup, v_dup, hv, hk, hlf, hq, hg, four, gate = _inproj(
            xs, mods[l], norm1_g[l].reshape(1, d), w_in[l].astype(BF16),
            jnp.tile(q_norm_g[l], 2).reshape(1, LANES), jnp.tile(k_norm_g[l], 2).reshape(1, LANES),
            cos, sin, lb_logits, layer=l)
        o_attn = _attention(q, k_dup, v_dup, attn_sink[l])
        o_hf, o_hb = _hgrn(hq, hk, hlf, hv)
        o_four = _fourier(four, wc, cs_lat, cs_ctx)
        xs = _merge_mlp(xs, mods[l], o_four, o_attn, o_hf, o_hb, hg, gate,
                        hgrn_norm_g[l].reshape(1, HGRN_DK), norm2_g[l].reshape(1, d),
                        w_branch[l].astype(BF16), w_out[l].astype(BF16),
                        w_ff1[l].astype(BF16), w_ff2[l].astype(BF16), latent_only=(l == depth - 1))
    return xs
```

```python
import functools
import math

import jax
import jax.numpy as jnp
from jax import lax
from jax.experimental import pallas as pl
from jax.experimental.pallas import tpu as pltpu

F32 = jnp.float32
BF16 = jnp.bfloat16

D_MODEL = 1024
SEQ = 2048
CTX_LEN = 256
TOKENS = SEQ + CTX_LEN
GRID_W = 64
EPS = 1e-6
NEG_BIG = -1e30
LB_FLOOR = 1e-30
ROPE_THETA = 10000.0

BRANCH_WIDTH = D_MODEL // 2
HEAD_DIM = 64
ATTN_HEADS = BRANCH_WIDTH // HEAD_DIM
ATTN_KV_HEADS = ATTN_HEADS // 4
KV_WIDTH = ATTN_KV_HEADS * HEAD_DIM
ATTN_BLOCK = 128
FOURIER_GROUPS = 4
FOURIER_GROUP_DIM = BRANCH_WIDTH // FOURIER_GROUPS
HGRN_HEADS = 4
HGRN_DK = BRANCH_WIDTH // HGRN_HEADS
D_FF = 4 * D_MODEL
N_BRANCH = 3

_SEG = (KV_WIDTH, KV_WIDTH, BRANCH_WIDTH, BRANCH_WIDTH, BRANCH_WIDTH,
        BRANCH_WIDTH, BRANCH_WIDTH, BRANCH_WIDTH, BRANCH_WIDTH, N_BRANCH * D_MODEL)
_OFF = [sum(_SEG[:i]) for i in range(len(_SEG) + 1)]
D_IN = _OFF[-1]

LANES = 128
VMEM_LIMIT_BYTES = 56 * 1024 * 1024

ROW_TILE = 576
ROW_TILES = TOKENS // ROW_TILE
LATENT_ROW_TILE = 512
assert ROW_TILES * ROW_TILE == TOKENS and SEQ % LATENT_ROW_TILE == 0
HGRN_CHUNK = 64
HGRN_CHUNKS = TOKENS // HGRN_CHUNK
HGRN_CTX_CHUNKS = CTX_LEN // HGRN_CHUNK
HGRN_BATCH = 8
EXP2_CLAMP = 100.0
LOG2E = 1.0 / math.log(2.0)
MOD_ROWS = 16


def _const_spec(shape):
    nd = len(shape)
    return pl.BlockSpec(shape, lambda *_: (0,) * nd, pipeline_mode=pl.Buffered(1))


def _layer_spec(stacked_shape, layer):
    rest = tuple(stacked_shape[1:])
    return pl.BlockSpec((None,) + rest, lambda *_: (layer,) + (0,) * len(rest), pipeline_mode=pl.Buffered(1))


def _params(semantics, flags=None):
    return pltpu.CompilerParams(dimension_semantics=semantics, vmem_limit_bytes=VMEM_LIMIT_BYTES,
                                flags=flags)


def _dot(a, b):
    return jnp.dot(a, b, preferred_element_type=F32)


def _dot_nt(a, b):
    return lax.dot_general(a, b, (((1,), (1,)), ((), ())), preferred_element_type=F32)


def _dot_tn(a, b):
    return lax.dot_general(a, b, (((0,), (0,)), ((), ())), preferred_element_type=F32)


def _sigmoid(x):
    return 1.0 / (1.0 + jnp.exp(-x))


def _silu(x):
    return x * _sigmoid(x)


def _mod_specs(layer):
    one = lambda index: pl.BlockSpec((None, None, 1, 6 * D_MODEL), index)
    return [one(lambda b, t: (layer, b, 0, 0)), one(lambda b, t: (layer, MOD_ROWS // 2, 0, 0))]


def _mod_picker(mod_ref, ctx_mod_ref, rows):
    row = pl.program_id(1) * rows + lax.broadcasted_iota(jnp.int32, (rows, 1), 0)
    is_ctx = row >= SEQ

    def pick(i):
        cols = slice(i * D_MODEL, (i + 1) * D_MODEL)
        return jnp.where(is_ctx, ctx_mod_ref[:, cols], mod_ref[:, cols])

    return pick


MOD_COL_TILE = 1536


def _modulation_kernel(cond_ref, w_ref, b_ref, out_ref):
    a = _silu(cond_ref[...]).astype(BF16)
    out_ref[...] = _dot(a, w_ref[...].astype(BF16)) + b_ref[...]


def _modulation(cond, w_mod, b_mod):
    depth = w_mod.shape[0]
    n = w_mod.shape[2]
    return pl.pallas_call(
        _modulation_kernel,
        out_shape=jax.ShapeDtypeStruct((depth, MOD_ROWS, n), F32),
        grid=(depth, n // MOD_COL_TILE),
        in_specs=[
            pl.BlockSpec((MOD_ROWS, D_MODEL), lambda l, j: (0, 0)),
            pl.BlockSpec((None, D_MODEL, MOD_COL_TILE), lambda l, j: (l, 0, j)),
            pl.BlockSpec((None, 1, MOD_COL_TILE), lambda l, j: (l, 0, j)),
        ],
        out_specs=pl.BlockSpec((None, MOD_ROWS, MOD_COL_TILE), lambda l, j: (l, 0, j)),
        compiler_params=_params(("parallel", "parallel")),
        name="modulation",
    )(cond, w_mod, b_mod.reshape(depth, 1, n))


def _lane_index(shape):
    return lax.broadcasted_iota(jnp.int32, shape, len(shape) - 1)


def _head_mean_matrix():
    shift = HEAD_DIM.bit_length() - 1
    r = lax.broadcasted_iota(jnp.int32, (LANES, LANES), 0) >> shift
    c = lax.broadcasted_iota(jnp.int32, (LANES, LANES), 1) >> shift
    return jnp.where(r == c, 1.0 / HEAD_DIM, 0.0).astype(BF16)


def _head_rms_norm(u, gain, mean_bd):
    ms = _dot((u * u).astype(BF16), mean_bd)
    return u * lax.rsqrt(ms + EPS) * gain


def _rope(u, cos, sin_signed):
    first = (_lane_index(u.shape) & 31) < 16
    partner = jnp.where(first, pltpu.roll(u, LANES - 16, axis=1), pltpu.roll(u, 16, axis=1))
    return u * cos + partner * sin_signed


def _hgrn_lower_bound(lb_ref, layer, direction):
    depth = lb_ref.shape[0] // 2
    rows = [lb_ref[2 * j + direction:2 * j + direction + 1, :] for j in range(depth)]
    m = functools.reduce(jnp.maximum, rows)
    e = [jnp.exp(r - m) for r in rows]
    z = functools.reduce(lambda a, b: a + b, e)
    p = [x / z for x in e]
    csum = functools.reduce(lambda a, b: a + b, p[:layer + 1])
    return csum - p[0]


def _hgrn_forget(z, lb):
    t = jnp.exp(-jnp.abs(z))
    pos = z >= 0.0
    lb_floor = jnp.maximum(lb, LB_FLOOR)
    r = 1.0 / (1.0 + t)
    k = (1.0 - lb) * jnp.where(pos, t * r, r)
    f = jnp.where(pos, 1.0 + lb_floor * t, t + lb_floor) * r
    return k, jnp.log(f) * LOG2E


GATE_CHUNKS = 6


def _inproj_kernel(x_ref, mod_ref, ctx_mod_ref, g1_ref, w_ref, qg_ref, kg_ref, cos_ref, sin_ref, lb_ref,
                   q_ref, k_ref, v_ref, hv_ref, hk_ref, hlf_ref, hq_ref, hg_ref, four_ref, gate_ref,
                   *, layer):
    x = x_ref[0]
    mod = _mod_picker(mod_ref, ctx_mod_ref, x.shape[0])
    ms = jnp.mean(x * x, axis=-1, keepdims=True)
    h = (x * lax.rsqrt(ms + EPS) * (g1_ref[...] * (1.0 + mod(1))) + mod(0)).astype(BF16)

    def proj(seg, width=None):
        a = _OFF[seg]
        b = _OFF[seg + 1] if width is None else a + width
        return _dot(h, w_ref[:, a:b])

    gate_width = N_BRANCH * D_MODEL // GATE_CHUNKS

    def gate_chunk(i):
        a = _OFF[9] + i * gate_width
        gate_ref[0, :, i * gate_width:(i + 1) * gate_width] = _dot(h, w_ref[:, a:a + gate_width]).astype(BF16)

    cos = cos_ref[...]
    sin = sin_ref[...]
    low_head = _lane_index((x.shape[0], LANES)) < HEAD_DIM
    mean_bd = _head_mean_matrix()

    kv = proj(0, 2 * KV_WIDTH)
    gate_chunk(0)
    kr = _rope(_head_rms_norm(kv[:, :KV_WIDTH], kg_ref[...], mean_bd), cos, sin)
    kr_sw = pltpu.roll(kr, HEAD_DIM, axis=1)
    k_ref[0, 0] = jnp.where(low_head, kr, kr_sw).astype(BF16)
    k_ref[0, 1] = jnp.where(low_head, kr_sw, kr).astype(BF16)
    vr = kv[:, KV_WIDTH:]
    vr_sw = pltpu.roll(vr, HEAD_DIM, axis=1)
    v_ref[0, 0] = jnp.where(low_head, vr, vr_sw).astype(BF16)
    v_ref[0, 1] = jnp.where(low_head, vr_sw, vr).astype(BF16)

    qr = proj(5)
    gate_chunk(1)
    q_gain = qg_ref[...] * (HEAD_DIM ** -0.5 * LOG2E)
    for c in range(BRANCH_WIDTH // LANES):
        u = qr[:, c * LANES:(c + 1) * LANES]
        q_ref[0, :, c * LANES:(c + 1) * LANES] = _rope(_head_rms_norm(u, q_gain, mean_bd), cos, sin).astype(BF16)

    for direction, seg in ((0, 3), (1, 4)):
        z = proj(seg)
        gate_chunk(2 + direction)
        k, log2f = _hgrn_forget(z, _hgrn_lower_bound(lb_ref, layer, direction))
        hk_ref[0, :, direction * BRANCH_WIDTH:(direction + 1) * BRANCH_WIDTH] = k.astype(BF16)
        hlf_ref[0, :, direction * BRANCH_WIDTH:(direction + 1) * BRANCH_WIDTH] = log2f
    qh = proj(6)
    gate_chunk(4)
    hq_ref[0] = _silu(qh).astype(BF16)
    hv_ref[0] = proj(2).astype(BF16)
    hg_ref[0] = proj(7).astype(BF16)
    four_ref[0] = proj(8).astype(BF16)
    gate_chunk(5)


def _inproj(xs, mods, g1, w_in, qg, kg, cos, sin, lb_logits, *, layer):
    bsz = xs.shape[0]
    row = lambda w: pl.BlockSpec((1, ROW_TILE, w), lambda b, t: (b, t, 0))
    dup = pl.BlockSpec((1, ATTN_KV_HEADS, ROW_TILE, LANES), lambda b, t: (b, 0, t, 0))
    tok = lambda w, dt: jax.ShapeDtypeStruct((bsz, TOKENS, w), dt)
    dup_shape = jax.ShapeDtypeStruct((bsz, ATTN_KV_HEADS, TOKENS, LANES), BF16)
    return pl.pallas_call(
        functools.partial(_inproj_kernel, layer=layer),
        out_shape=(tok(BRANCH_WIDTH, BF16), dup_shape, dup_shape,
                   tok(BRANCH_WIDTH, BF16), tok(2 * BRANCH_WIDTH, BF16), tok(2 * BRANCH_WIDTH, F32),
                   tok(BRANCH_WIDTH, BF16), tok(BRANCH_WIDTH, BF16), tok(BRANCH_WIDTH, BF16),
                   tok(N_BRANCH * D_MODEL, BF16)),
        grid=(bsz, ROW_TILES),
        in_specs=[
            row(D_MODEL),
            *_mod_specs(layer),
            _layer_spec(g1.shape, layer),
            _layer_spec(w_in.shape, layer),
            _layer_spec(qg.shape, layer),
            _layer_spec(kg.shape, layer),
            pl.BlockSpec((ROW_TILE, LANES), lambda b, t: (t, 0)),
            pl.BlockSpec((ROW_TILE, LANES), lambda b, t: (t, 0)),
            _const_spec(lb_logits.shape),
        ],
        out_specs=(row(BRANCH_WIDTH), dup, dup,
                   row(BRANCH_WIDTH), row(2 * BRANCH_WIDTH), row(2 * BRANCH_WIDTH),
                   row(BRANCH_WIDTH), row(BRANCH_WIDTH), row(BRANCH_WIDTH),
                   row(N_BRANCH * D_MODEL)),
        compiler_params=_params(("parallel", "parallel")),
        name="inproj",
    )(xs, mods, mods, g1, w_in, qg, kg, cos, sin, lb_logits)


ATTN_BLOCKS = TOKENS // ATTN_BLOCK
LATENT_ATTN_BLOCKS = SEQ // ATTN_BLOCK
GROUP = ATTN_HEADS // ATTN_KV_HEADS
PAIR_COLS = 2 * ATTN_BLOCK
ATTN_STEP_BLOCKS = 4


def _attend_heads(sink_ref, q_ref, o_ref, query_blocks):
    low_head = _lane_index((ATTN_BLOCK, LANES)) < HEAD_DIM
    zero = jnp.zeros((), BF16)
    odd_cols = lax.broadcasted_iota(jnp.int32, (1, PAIR_COLS), 1) >= ATTN_BLOCK
    tasks = []
    for n, (kv_blocks, masks) in enumerate(query_blocks):
        kv = [kv_blocks(hk) for hk in range(ATTN_KV_HEADS)]
        rows = slice(n * ATTN_BLOCK, (n + 1) * ATTN_BLOCK)
        tasks += [dict(pair=c, rows=rows, kv=kv[(2 * c) // GROUP], masks=masks) for c in range(ATTN_HEADS // 2)]

    def scores_stage(st):
        c = st["pair"]
        tile = q_ref[0, st["rows"], c * LANES:(c + 1) * LANES]
        qs = jnp.concatenate([jnp.where(low_head, tile, zero), jnp.where(low_head, zero, tile)], axis=0)
        scores = [_dot_nt(kb, qs) for kb in st["kv"][0]]
        st["scores"] = [s if mk is None else jnp.where(mk, s, NEG_BIG) for s, mk in zip(scores, st["masks"])]
        st["sink"] = jnp.where(odd_cols, sink_ref[2 * c + 1], sink_ref[2 * c]) * LOG2E

    def max_stage(st):
        m = st["sink"]
        for s in st["scores"]:
            m = jnp.maximum(m, jnp.max(s, axis=0, keepdims=True))
        st["m"] = m

    def values_stage(st):
        denom = jnp.exp2(st["sink"] - st["m"])
        o = None
        for s, vb in zip(st["scores"], st["kv"][1]):
            p = jnp.exp2(s - st["m"])
            denom = denom + jnp.sum(p, axis=0, keepdims=True)
            pv = _dot_tn(vb, p.astype(BF16))
            o = pv if o is None else o + pv
        o = o / denom
        c = st["pair"]
        o_ref[0, st["rows"], c * LANES:(c + 1) * LANES] = jnp.where(
            low_head, o[:, :ATTN_BLOCK].T, o[:, ATTN_BLOCK:].T).astype(BF16)

    stages = (scores_stage, max_stage, values_stage)
    for tick in range(len(tasks) + len(stages) - 1):
        for depth, stage in enumerate(stages):
            if 0 <= tick - depth < len(tasks):
                stage(tasks[tick - depth])


def _attention_kernel(sink_ref, q_ref, k_ref, v_ref, o_ref):
    j2 = lax.broadcasted_iota(jnp.int32, (2 * ATTN_BLOCK, PAIR_COLS), 0)
    t2 = lax.broadcasted_iota(jnp.int32, (2 * ATTN_BLOCK, PAIR_COLS), 1) & (ATTN_BLOCK - 1)
    j1 = lax.broadcasted_iota(jnp.int32, (ATTN_BLOCK, PAIR_COLS), 0)
    t1 = lax.broadcasted_iota(jnp.int32, (ATTN_BLOCK, PAIR_COLS), 1) & (ATTN_BLOCK - 1)

    def query_block(n):
        i = pl.program_id(1) * ATTN_STEP_BLOCKS + n
        prev_start = pl.multiple_of(jnp.maximum(i - 1, 0) * ATTN_BLOCK, ATTN_BLOCK)
        cur_start = pl.multiple_of(i * ATTN_BLOCK, ATTN_BLOCK)
        next_start = pl.multiple_of(jnp.minimum(i + 1, LATENT_ATTN_BLOCKS - 1) * ATTN_BLOCK, ATTN_BLOCK)

        def blocks(ref, hk):
            near = jnp.concatenate([ref[0, hk, pl.ds(prev_start, ATTN_BLOCK), :],
                                    ref[0, hk, pl.ds(cur_start, ATTN_BLOCK), :]], axis=0)
            return near, ref[0, hk, pl.ds(next_start, ATTN_BLOCK), :], ref[0, hk, SEQ:TOKENS, :]

        near_valid = jnp.logical_or(j2 >= ATTN_BLOCK, jnp.logical_and(j2 >= t2, i > 0))
        next_valid = jnp.logical_and(j1 <= t1, i < LATENT_ATTN_BLOCKS - 1)
        return (lambda hk: (blocks(k_ref, hk), blocks(v_ref, hk))), (near_valid, next_valid, None)

    _attend_heads(sink_ref, q_ref, o_ref, [query_block(n) for n in range(ATTN_STEP_BLOCKS)])


def _context_attention_kernel(sink_ref, q_ref, kx_ref, vx_ref, o_in_ref, o_ref):
    del o_in_ref
    whole_context = (lambda hk: ((kx_ref[0, hk],), (vx_ref[0, hk],))), (None,)
    _attend_heads(sink_ref, q_ref, o_ref, [whole_context] * (CTX_LEN // ATTN_BLOCK))


def _attention(q, k_dup, v_dup, sink):
    bsz = q.shape[0]
    stream = pl.BlockSpec((1, ATTN_KV_HEADS, TOKENS, LANES), lambda b, i: (b, 0, 0, 0))
    ctx = pl.BlockSpec((1, ATTN_KV_HEADS, CTX_LEN, LANES), lambda b, i: (b, 0, SEQ // CTX_LEN, 0))
    smem = pl.BlockSpec(memory_space=pltpu.SMEM)
    out_shape = jax.ShapeDtypeStruct((bsz, TOKENS, BRANCH_WIDTH), BF16)
    step_rows = pl.BlockSpec((1, ATTN_STEP_BLOCKS * ATTN_BLOCK, BRANCH_WIDTH), lambda b, i: (b, i, 0))
    o_lat = pl.pallas_call(
        _attention_kernel,
        out_shape=out_shape,
        grid=(bsz, LATENT_ATTN_BLOCKS // ATTN_STEP_BLOCKS),
        in_specs=[smem, step_rows, stream, stream],
        out_specs=step_rows,
        compiler_params=_params(("parallel", "arbitrary")),
        name="attention",
    )(sink, q, k_dup, v_dup)
    ctx_rows = pl.BlockSpec((1, CTX_LEN, BRANCH_WIDTH), lambda b, i: (b, SEQ // CTX_LEN, 0))
    return pl.pallas_call(
        _context_attention_kernel,
        out_shape=out_shape,
        grid=(bsz, 1),
        in_specs=[smem, ctx_rows, ctx, ctx, pl.BlockSpec(memory_space=pl.ANY)],
        out_specs=ctx_rows,
        input_output_aliases={4: 0},
        compiler_params=_params(("parallel", "arbitrary")),
        name="context_attention",
    )(sink, q, k_dup, v_dup, o_lat)


def _cumsum_rows(tri, x):
    hi = x.astype(BF16)
    lo = (x - hi.astype(F32)).astype(BF16)
    return _dot(tri, hi) + _dot(tri, lo)


def _hgrn_kernel(qf_ref, kf_ref, lf_ref, vf_ref, qb_ref, kb_ref, lb_ref, vb_ref, of_ref, ob_ref, state_ref):
    @pl.when(pl.program_id(1) == 0)
    def _():
        state_ref[...] = jnp.zeros(state_ref.shape, F32)

    n = HGRN_CHUNK
    rr = lax.broadcasted_iota(jnp.int32, (n, n), 0)
    cc = lax.broadcasted_iota(jnp.int32, (n, n), 1)
    directions = ((0, cc <= rr, n - 1, n // 2 - 1, qf_ref, kf_ref, lf_ref, vf_ref, of_ref),
                  (1, cc >= rr, 0, n // 2, qb_ref, kb_ref, lb_ref, vb_ref, ob_ref))

    chains = []
    for slot, causal, last_row, mid_row, q_ref, k_ref, lf_ref_d, v_ref, o_ref in directions:
        tri = jnp.where(causal, 1.0, 0.0).astype(BF16)
        for bi in range(HGRN_BATCH):
            g_all = _cumsum_rows(tri, lf_ref_d[bi])
            for hd in range(HGRN_HEADS):
                ln = slice(hd * HGRN_DK, (hd + 1) * HGRN_DK)
                g = g_all[:, ln]
                chains.append(dict(
                    causal=causal, g=g, g_tot=g[last_row:last_row + 1, :], g_mid=g[mid_row:mid_row + 1, :],
                    q=q_ref[bi, :, ln], k=k_ref[bi, :, ln], v=v_ref[bi, :, ln],
                    o_ref=o_ref, state=(bi, slot, hd), out=(bi, slice(None), ln)))

    for ch in chains:
        d = ch["g"] - ch["g_mid"]
        ch["qa"] = ch["q"] * jnp.exp2(jnp.minimum(d, EXP2_CLAMP)).astype(BF16)
        ch["kb"] = ch["k"] * jnp.exp2(jnp.minimum(-d, EXP2_CLAMP)).astype(BF16)
        ch["st_mid"] = state_ref[ch["state"]] * jnp.exp2(ch["g_mid"])
    for ch in chains:
        ch["a"] = _dot_nt(ch["qa"], ch["kb"])
        ch["o_inter"] = _dot_nt(ch["qa"], ch["st_mid"].astype(BF16))
        ch["u"] = _dot_tn(ch["v"], ch["kb"])
    for ch in chains:
        a = jnp.where(ch["causal"], ch["a"], 0.0).astype(BF16)
        ch["o_ref"][ch["out"]] = (ch["o_inter"] + _dot(a, ch["v"])).astype(ch["o_ref"].dtype)
        state_ref[ch["state"]] = (ch["st_mid"] + ch["u"]) * jnp.exp2(ch["g_tot"] - ch["g_mid"])


def _hgrn(hq, hk, hlf, hv):
    bsz = hq.shape[0]
    assert bsz % HGRN_BATCH == 0
    lat = HGRN_CHUNKS - HGRN_CTX_CHUNKS

    def fwd(c):
        return jnp.where(c < HGRN_CTX_CHUNKS, lat + c, c - HGRN_CTX_CHUNKS)

    def bwd(c):
        return HGRN_CHUNKS - 1 - c

    def spec(order, lane_block):
        return pl.BlockSpec((HGRN_BATCH, HGRN_CHUNK, BRANCH_WIDTH), lambda b, c: (b, order(c), lane_block))

    out = jax.ShapeDtypeStruct((bsz, TOKENS, BRANCH_WIDTH), BF16)
    return pl.pallas_call(
        _hgrn_kernel,
        out_shape=(out, out),
        grid=(bsz // HGRN_BATCH, HGRN_CHUNKS),
        in_specs=[spec(fwd, 0), spec(fwd, 0), spec(fwd, 0), spec(fwd, 0),
                  spec(bwd, 0), spec(bwd, 1), spec(bwd, 1), spec(bwd, 0)],
        out_specs=(spec(fwd, 0), spec(bwd, 0)),
        scratch_shapes=[pltpu.VMEM((HGRN_BATCH, 2, HGRN_HEADS, HGRN_DK, HGRN_DK), F32)],
        compiler_params=_params(("parallel", "arbitrary")),
        name="hgrn",
    )(hq, hk, hlf, hv, hq, hk, hlf, hv)


FOURIER_ROW_TILE = 1024


def _fourier_kernel(x_ref, wc_ref, cs_ref, o_ref, uv_ref):
    t = pl.program_id(1)

    @pl.when(t == 0)
    def _():
        uv = _dot(x_ref[0], wc_ref[...])
        uv_ref[0:SEQ, :] = uv[:, :BRANCH_WIDTH].astype(BF16)
        uv_ref[SEQ:2 * SEQ, :] = uv[:, BRANCH_WIDTH:].astype(BF16)

    r0 = pl.multiple_of(t * FOURIER_ROW_TILE, FOURIER_ROW_TILE)
    y = _dot(cs_ref[pl.ds(r0, FOURIER_ROW_TILE), :], uv_ref[...])
    o_ref[0] = (y * (1.0 / math.sqrt(SEQ * FOURIER_GROUP_DIM))).astype(BF16)


def _context_fourier_kernel(x_ref, wc_ref, cs_ref, o_in_ref, o_ref):
    del o_in_ref
    uv = _dot(x_ref[0], wc_ref[...])
    uvc = jnp.concatenate([uv[:, :BRANCH_WIDTH], uv[:, BRANCH_WIDTH:]], axis=0).astype(BF16)
    y = _dot(cs_ref[...], uvc)
    o_ref[0] = (y * (1.0 / math.sqrt(CTX_LEN * FOURIER_GROUP_DIM))).astype(BF16)


def _fourier(four, wc, cs_lat, cs_ctx):
    bsz = four.shape[0]
    out_shape = jax.ShapeDtypeStruct((bsz, TOKENS, BRANCH_WIDTH), BF16)
    o_lat = pl.pallas_call(
        _fourier_kernel,
        out_shape=out_shape,
        grid=(bsz, SEQ // FOURIER_ROW_TILE),
        in_specs=[pl.BlockSpec((1, SEQ, BRANCH_WIDTH), lambda b, t: (b, 0, 0)),
                  _const_spec(wc.shape), _const_spec(cs_lat.shape)],
        out_specs=pl.BlockSpec((1, FOURIER_ROW_TILE, BRANCH_WIDTH), lambda b, t: (b, t, 0)),
        scratch_shapes=[pltpu.VMEM((2 * SEQ, BRANCH_WIDTH), BF16)],
        compiler_params=_params(("parallel", "arbitrary")),
        name="fourier",
    )(four, wc, cs_lat)
    ctx_rows = pl.BlockSpec((1, CTX_LEN, BRANCH_WIDTH), lambda b: (b, SEQ // CTX_LEN, 0))
    return pl.pallas_call(
        _context_fourier_kernel,
        out_shape=out_shape,
        grid=(bsz,),
        in_specs=[ctx_rows, _const_spec(wc.shape), _const_spec(cs_ctx.shape),
                  pl.BlockSpec(memory_space=pl.ANY)],
        out_specs=ctx_rows,
        input_output_aliases={3: 0},
        compiler_params=_params(("parallel",)),
        name="context_fourier",
    )(four, wc, cs_ctx, o_lat)


def _dft_tables(n, split):
    col = jnp.arange(n, dtype=jnp.int32)

    def trig(row_factor):
        ang = ((row_factor[:, None] * col[None, :]) % n).astype(F32) * (2.0 * math.pi / n)
        return jnp.cos(ang), jnp.sin(ang)

    cos_a, sin_a = trig(split * jnp.arange(n // split, dtype=jnp.int32))
    cos_b, sin_b = trig(jnp.arange(split, dtype=jnp.int32))
    cos = cos_a[:, None, :] * cos_b[None, :, :] - sin_a[:, None, :] * sin_b[None, :, :]
    sin = sin_a[:, None, :] * cos_b[None, :, :] + cos_a[:, None, :] * sin_b[None, :, :]
    return jnp.concatenate([cos.reshape(n, n), -sin.reshape(n, n)], axis=1).astype(BF16)


def _channel_dft_table():
    c = jnp.arange(BRANCH_WIDTH, dtype=jnp.int32)
    same = (c[:, None] // FOURIER_GROUP_DIM) == (c[None, :] // FOURIER_GROUP_DIM)
    ang = (((c[:, None] % FOURIER_GROUP_DIM) * (c[None, :] % FOURIER_GROUP_DIM)) % FOURIER_GROUP_DIM
           ).astype(F32) * (2.0 * math.pi / FOURIER_GROUP_DIM)
    cos = jnp.where(same, jnp.cos(ang), 0.0)
    sin = jnp.where(same, jnp.sin(ang), 0.0)
    return jnp.concatenate([cos, sin], axis=1).astype(BF16)


FF_CHUNK = 1024


def _rms(x):
    return x * lax.rsqrt(jnp.mean(x * x, axis=-1, keepdims=True) + EPS)


def _merge_mlp_kernel(x_ref, mod_ref, ctx_mod_ref, of_ref, oa_ref, hf_ref, hb_ref, hg_ref, gate_ref,
                      hng_ref, g2_ref, wb_ref, wo_ref, w1_ref, w2_ref, out_ref):
    d = D_MODEL
    x = x_ref[0]
    mod = _mod_picker(mod_ref, ctx_mod_ref, x.shape[0])

    o_h = hf_ref[0].astype(F32) + hb_ref[0].astype(F32)
    o_n = jnp.concatenate(
        [_rms(o_h[:, hd * HGRN_DK:(hd + 1) * HGRN_DK]) * hng_ref[...] for hd in range(HGRN_HEADS)], axis=-1)
    o_hr = (o_n * _silu(hg_ref[0].astype(F32))).astype(BF16)

    branches = (of_ref[0], oa_ref[0], o_hr)
    mix = None
    for n, br in enumerate(branches):
        term = _sigmoid(gate_ref[0, :, n * d:(n + 1) * d].astype(F32)) * _dot(br, wb_ref[n])
        mix = term if mix is None else mix + term
    y = _dot(mix.astype(BF16), wo_ref[...])
    x1 = x + mod(2) * y

    h2 = (_rms(x1) * (g2_ref[...] * (1.0 + mod(4))) + mod(3)).astype(BF16)
    ff = None
    for c in range(D_FF // FF_CHUNK):
        cols = slice(c * FF_CHUNK, (c + 1) * FF_CHUNK)
        u = jnp.square(jnp.maximum(_dot(h2, w1_ref[:, cols]), 0.0)).astype(BF16)
        part = _dot(u, w2_ref[cols, :])
        ff = part if ff is None else ff + part
    out_ref[0] = x1 + mod(5) * ff


def _merge_mlp(xs, mods, o_four, o_attn, o_hf, o_hb, hg, gate, hng, g2, wb, wo, w1, w2, *, layer, latent_only):
    bsz = xs.shape[0]
    rows = LATENT_ROW_TILE if latent_only else ROW_TILE
    n_rows = SEQ if latent_only else TOKENS
    row = lambda w: pl.BlockSpec((1, rows, w), lambda b, t: (b, t, 0))
    return pl.pallas_call(
        _merge_mlp_kernel,
        out_shape=jax.ShapeDtypeStruct((bsz, n_rows, D_MODEL), F32),
        grid=(bsz, n_rows // rows),
        in_specs=[row(D_MODEL),
                  *_mod_specs(layer),
                  row(BRANCH_WIDTH), row(BRANCH_WIDTH), row(BRANCH_WIDTH), row(BRANCH_WIDTH),
                  row(BRANCH_WIDTH), row(N_BRANCH * D_MODEL),
                  *[_layer_spec(a.shape, layer) for a in (hng, g2, wb, wo, w1, w2)]],
        out_specs=row(D_MODEL),
        compiler_params=_params(("parallel", "parallel")),
        name="merge_mlp",
    )(xs, mods, mods, o_four, o_attn, o_hf, o_hb, hg, gate, hng, g2, wb, wo, w1, w2)


def _rope_tables():
    pos = jnp.arange(SEQ)
    row = (pos // GRID_W).astype(F32)
    col = (pos % GRID_W).astype(F32)
    axis_dim = HEAD_DIM // 2
    half = axis_dim // 2
    inv_freq = ROPE_THETA ** (-jnp.arange(0, axis_dim, 2, dtype=F32) / axis_dim)
    ang_r = row[:, None] * inv_freq
    ang_c = col[:, None] * inv_freq
    ang = jnp.concatenate([ang_r, ang_r, ang_c, ang_c], axis=1)
    sign = jnp.tile(jnp.concatenate([-jnp.ones(half, F32), jnp.ones(half, F32)]), 2)
    cos = jnp.concatenate([jnp.cos(ang), jnp.ones((CTX_LEN, HEAD_DIM), F32)], axis=0)
    sin = jnp.concatenate([jnp.sin(ang) * sign, jnp.zeros((CTX_LEN, HEAD_DIM), F32)], axis=0)
    return jnp.tile(cos, (1, 2)), jnp.tile(sin, (1, 2))


def kernel(x, c, ctx, c_ctx, w_mod, b_mod, norm1_g, norm2_g, w_in, q_norm_g, k_norm_g, attn_sink,
           hgrn_lb_logits, hgrn_norm_g, w_branch, w_out, w_ff1, w_ff2):
    bsz, n_tok, d = x.shape
    depth = w_mod.shape[0]
    assert (n_tok, d, ctx.shape[1]) == (SEQ, D_MODEL, CTX_LEN) and bsz <= MOD_ROWS // 2

    xs = jnp.concatenate([x, ctx], axis=1)
    cond = jnp.zeros((MOD_ROWS, d), F32).at[:bsz].set(c).at[MOD_ROWS // 2].set(c_ctx)
    mods = _modulation(cond, w_mod, b_mod).reshape(depth, MOD_ROWS, 1, 6 * d)

    cos, sin = _rope_tables()
    wc = _channel_dft_table()
    cs_lat = _dft_tables(SEQ, 32)
    cs_ctx = _dft_tables(CTX_LEN, 16)
    lb_logits = hgrn_lb_logits.reshape(depth * 2, BRANCH_WIDTH)

    g1 = norm1_g.reshape(depth, 1, d)
    g2 = norm2_g.reshape(depth, 1, d)
    qg = jnp.tile(q_norm_g, (1, 2)).reshape(depth, 1, LANES)
    kg = jnp.tile(k_norm_g, (1, 2)).reshape(depth, 1, LANES)
    hng = hgrn_norm_g.reshape(depth, 1, HGRN_DK)
    w_in, w_branch, w_out, w_ff1, w_ff2 = (w.astype(BF16) for w in (w_in, w_branch, w_out, w_ff1, w_ff2))

    for l in range(depth):
        q, k_dup, v_dup, hv, hk, hlf, hq, hg, four, gate = _inproj(
            xs, mods, g1, w_in, qg, kg, cos, sin, lb_logits, layer=l)
        o_attn = _attention(q, k_dup, v_dup, attn_sink[l])
        o_hf, o_hb = _hgrn(hq, hk, hlf, hv)
        o_four = _fourier(four, wc, cs_lat, cs_ctx)
        xs = _merge_mlp(xs, mods, o_four, o_attn, o_hf, o_hb, hg, gate, hng, g2,
                        w_branch, w_out, w_ff1, w_ff2, layer=l, latent_only=(l == depth - 1))
    return xs
```

```python
import functools
import math

import jax
import jax.numpy as jnp
from jax import lax
from jax.experimental import pallas as pl
from jax.experimental.pallas import tpu as pltpu

F32 = jnp.float32
BF16 = jnp.bfloat16

D_MODEL = 1024
SEQ = 2048
CTX_LEN = 256
TOKENS = SEQ + CTX_LEN
GRID_W = 64
EPS = 1e-6
NEG_BIG = -1e30
LB_FLOOR = 1e-30
ROPE_THETA = 10000.0

BRANCH_WIDTH = D_MODEL // 2
HEAD_DIM = 64
ATTN_HEADS = BRANCH_WIDTH // HEAD_DIM
ATTN_KV_HEADS = ATTN_HEADS // 4
KV_WIDTH = ATTN_KV_HEADS * HEAD_DIM
ATTN_BLOCK = 128
FOURIER_GROUPS = 4
FOURIER_GROUP_DIM = BRANCH_WIDTH // FOURIER_GROUPS
HGRN_HEADS = 4
HGRN_DK = BRANCH_WIDTH // HGRN_HEADS
D_FF = 4 * D_MODEL
N_BRANCH = 3

_SEG = (KV_WIDTH, KV_WIDTH, BRANCH_WIDTH, BRANCH_WIDTH, BRANCH_WIDTH,
        BRANCH_WIDTH, BRANCH_WIDTH, BRANCH_WIDTH, BRANCH_WIDTH, N_BRANCH * D_MODEL)
_OFF = [sum(_SEG[:i]) for i in range(len(_SEG) + 1)]
D_IN = _OFF[-1]

LANES = 128
VMEM_LIMIT_BYTES = 56 * 1024 * 1024

ROW_TILE = 576
ROW_TILES = TOKENS // ROW_TILE
LATENT_ROW_TILE = 512
assert ROW_TILES * ROW_TILE == TOKENS and SEQ % LATENT_ROW_TILE == 0
HGRN_CHUNK = 64
HGRN_CHUNKS = TOKENS // HGRN_CHUNK
HGRN_CTX_CHUNKS = CTX_LEN // HGRN_CHUNK
HGRN_BATCH = 8
EXP2_CLAMP = 100.0
LOG2E = 1.0 / math.log(2.0)
MOD_ROWS = 16


def _const_spec(shape):
    nd = len(shape)
    return pl.BlockSpec(shape, lambda *_: (0,) * nd, pipeline_mode=pl.Buffered(1))


def _layer_spec(stacked_shape, layer):
    rest = tuple(stacked_shape[1:])
    return pl.BlockSpec((None,) + rest, lambda *_: (layer,) + (0,) * len(rest), pipeline_mode=pl.Buffered(1))


def _params(semantics, flags=None):
    return pltpu.CompilerParams(dimension_semantics=semantics, vmem_limit_bytes=VMEM_LIMIT_BYTES,
                                flags=flags)


def _dot(a, b):
    return jnp.dot(a, b, preferred_element_type=F32)


def _dot_nt(a, b):
    return lax.dot_general(a, b, (((1,), (1,)), ((), ())), preferred_element_type=F32)


def _dot_tn(a, b):
    return lax.dot_general(a, b, (((0,), (0,)), ((), ())), preferred_element_type=F32)


def _sigmoid(x):
    return 1.0 / (1.0 + jnp.exp(-x))


def _silu(x):
    return x * _sigmoid(x)


def _mod_specs(layer):
    one = lambda index: pl.BlockSpec((None, None, 1, 6 * D_MODEL), index)
    return [one(lambda b, t: (layer, b, 0, 0)), one(lambda b, t: (layer, MOD_ROWS // 2, 0, 0))]


def _mod_picker(mod_ref, ctx_mod_ref, rows):
    row = pl.program_id(1) * rows + lax.broadcasted_iota(jnp.int32, (rows, 1), 0)
    is_ctx = row >= SEQ

    def pick(i):
        cols = slice(i * D_MODEL, (i + 1) * D_MODEL)
        return jnp.where(is_ctx, ctx_mod_ref[:, cols], mod_ref[:, cols])

    return pick


MOD_COL_TILE = 1536


def _modulation_kernel(cond_ref, w_ref, b_ref, out_ref):
    a = _silu(cond_ref[...]).astype(BF16)
    out_ref[...] = _dot(a, w_ref[...].astype(BF16)) + b_ref[...]


def _modulation(cond, w_mod, b_mod):
    depth = w_mod.shape[0]
    n = w_mod.shape[2]
    return pl.pallas_call(
        _modulation_kernel,
        out_shape=jax.ShapeDtypeStruct((depth, MOD_ROWS, n), F32),
        grid=(depth, n // MOD_COL_TILE),
        in_specs=[
            pl.BlockSpec((MOD_ROWS, D_MODEL), lambda l, j: (0, 0)),
            pl.BlockSpec((None, D_MODEL, MOD_COL_TILE), lambda l, j: (l, 0, j)),
            pl.BlockSpec((None, 1, MOD_COL_TILE), lambda l, j: (l, 0, j)),
        ],
        out_specs=pl.BlockSpec((None, MOD_ROWS, MOD_COL_TILE), lambda l, j: (l, 0, j)),
        compiler_params=_params(("parallel", "parallel")),
        name="modulation",
    )(cond, w_mod, b_mod.reshape(depth, 1, n))


def _lane_index(shape):
    return lax.broadcasted_iota(jnp.int32, shape, len(shape) - 1)


def _head_mean_matrix():
    shift = HEAD_DIM.bit_length() - 1
    r = lax.broadcasted_iota(jnp.int32, (LANES, LANES), 0) >> shift
    c = lax.broadcasted_iota(jnp.int32, (LANES, LANES), 1) >> shift
    return jnp.where(r == c, 1.0 / HEAD_DIM, 0.0).astype(BF16)


def _head_rms_norm(u, gain, mean_bd):
    ms = _dot((u * u).astype(BF16), mean_bd)
    return u * lax.rsqrt(ms + EPS) * gain


def _rope(u, cos, sin_signed):
    first = (_lane_index(u.shape) & 31) < 16
    partner = jnp.where(first, pltpu.roll(u, LANES - 16, axis=1), pltpu.roll(u, 16, axis=1))
    return u * cos + partner * sin_signed


def _hgrn_lower_bound(lb_ref, layer, direction):
    depth = lb_ref.shape[0] // 2
    rows = [lb_ref[2 * j + direction:2 * j + direction + 1, :] for j in range(depth)]
    m = functools.reduce(jnp.maximum, rows)
    e = [jnp.exp(r - m) for r in rows]
    z = functools.reduce(lambda a, b: a + b, e)
    p = [x / z for x in e]
    csum = functools.reduce(lambda a, b: a + b, p[:layer + 1])
    return csum - p[0]


def _hgrn_forget(z, lb):
    t = jnp.exp(-jnp.abs(z))
    pos = z >= 0.0
    lb_floor = jnp.maximum(lb, LB_FLOOR)
    r = 1.0 / (1.0 + t)
    k = (1.0 - lb) * jnp.where(pos, t * r, r)
    f = jnp.where(pos, 1.0 + lb_floor * t, t + lb_floor) * r
    return k, jnp.log(f) * LOG2E


GATE_CHUNKS = 6


def _inproj_kernel(x_ref, mod_ref, ctx_mod_ref, g1_ref, w_ref, qg_ref, kg_ref, cos_ref, sin_ref, lb_ref,
                   q_ref, k_ref, v_ref, hv_ref, hk_ref, hlf_ref, hq_ref, hg_ref, four_ref, gate_ref,
                   *, layer):
    x = x_ref[0]
    mod = _mod_picker(mod_ref, ctx_mod_ref, x.shape[0])
    ms = jnp.mean(x * x, axis=-1, keepdims=True)
    h = (x * lax.rsqrt(ms + EPS) * (g1_ref[...] * (1.0 + mod(1))) + mod(0)).astype(BF16)

    def proj(seg, width=None):
        a = _OFF[seg]
        b = _OFF[seg + 1] if width is None else a + width
        return _dot(h, w_ref[:, a:b])

    gate_width = N_BRANCH * D_MODEL // GATE_CHUNKS

    def gate_chunk(i):
        a = _OFF[9] + i * gate_width
        gate_ref[0, :, i * gate_width:(i + 1) * gate_width] = _dot(h, w_ref[:, a:a + gate_width]).astype(BF16)

    cos = cos_ref[...]
    sin = sin_ref[...]
    low_head = _lane_index((x.shape[0], LANES)) < HEAD_DIM
    mean_bd = _head_mean_matrix()

    kv = proj(0, 2 * KV_WIDTH)
    gate_chunk(0)
    kr = _rope(_head_rms_norm(kv[:, :KV_WIDTH], kg_ref[...], mean_bd), cos, sin)
    kr_sw = pltpu.roll(kr, HEAD_DIM, axis=1)
    k_ref[0, 0] = jnp.where(low_head, kr, kr_sw).astype(BF16)
    k_ref[0, 1] = jnp.where(low_head, kr_sw, kr).astype(BF16)
    vr = kv[:, KV_WIDTH:]
    vr_sw = pltpu.roll(vr, HEAD_DIM, axis=1)
    v_ref[0, 0] = jnp.where(low_head, vr, vr_sw).astype(BF16)
    v_ref[0, 1] = jnp.where(low_head, vr_sw, vr).astype(BF16)

    qr = proj(5)
    gate_chunk(1)
    q_gain = qg_ref[...] * (HEAD_DIM ** -0.5 * LOG2E)
    for c in range(BRANCH_WIDTH // LANES):
        u = qr[:, c * LANES:(c + 1) * LANES]
        q_ref[0, :, c * LANES:(c + 1) * LANES] = _rope(_head_rms_norm(u, q_gain, mean_bd), cos, sin).astype(BF16)

    for direction, seg in ((0, 3), (1, 4)):
        z = proj(seg)
        gate_chunk(2 + direction)
        k, log2f = _hgrn_forget(z, _hgrn_lower_bound(lb_ref, layer, direction))
        hk_ref[0, :, direction * BRANCH_WIDTH:(direction + 1) * BRANCH_WIDTH] = k.astype(BF16)
        hlf_ref[0, :, direction * BRANCH_WIDTH:(direction + 1) * BRANCH_WIDTH] = log2f
    qh = proj(6)
    gate_chunk(4)
    hq_ref[0] = _silu(qh).astype(BF16)
    hv_ref[0] = proj(2).astype(BF16)
    hg_ref[0] = proj(7).astype(BF16)
    four_ref[0] = proj(8).astype(BF16)
    gate_chunk(5)


def _inproj(xs, mods, g1, w_in, qg, kg, cos, sin, lb_logits, *, layer):
    bsz = xs.shape[0]
    row = lambda w: pl.BlockSpec((1, ROW_TILE, w), lambda b, t: (b, t, 0))
    dup = pl.BlockSpec((1, ATTN_KV_HEADS, ROW_TILE, LANES), lambda b, t: (b, 0, t, 0))
    tok = lambda w, dt: jax.ShapeDtypeStruct((bsz, TOKENS, w), dt)
    dup_shape = jax.ShapeDtypeStruct((bsz, ATTN_KV_HEADS, TOKENS, LANES), BF16)
    return pl.pallas_call(
        functools.partial(_inproj_kernel, layer=layer),
        out_shape=(tok(BRANCH_WIDTH, BF16), dup_shape, dup_shape,
                   tok(BRANCH_WIDTH, BF16), tok(2 * BRANCH_WIDTH, BF16), tok(2 * BRANCH_WIDTH, F32),
                   tok(BRANCH_WIDTH, BF16), tok(BRANCH_WIDTH, BF16), tok(BRANCH_WIDTH, BF16),
                   tok(N_BRANCH * D_MODEL, BF16)),
        grid=(bsz, ROW_TILES),
        in_specs=[
            row(D_MODEL),
            *_mod_specs(layer),
            _layer_spec(g1.shape, layer),
            _layer_spec(w_in.shape, layer),
            _layer_spec(qg.shape, layer),
            _layer_spec(kg.shape, layer),
            pl.BlockSpec((ROW_TILE, LANES), lambda b, t: (t, 0)),
            pl.BlockSpec((ROW_TILE, LANES), lambda b, t: (t, 0)),
            _const_spec(lb_logits.shape),
        ],
        out_specs=(row(BRANCH_WIDTH), dup, dup,
                   row(BRANCH_WIDTH), row(2 * BRANCH_WIDTH), row(2 * BRANCH_WIDTH),
                   row(BRANCH_WIDTH), row(BRANCH_WIDTH), row(BRANCH_WIDTH),
                   row(N_BRANCH * D_MODEL)),
        compiler_params=_params(("parallel", "parallel")),
        name="inproj",
    )(xs, mods, mods, g1, w_in, qg, kg, cos, sin, lb_logits)


ATTN_BLOCKS = TOKENS // ATTN_BLOCK
LATENT_ATTN_BLOCKS = SEQ // ATTN_BLOCK
GROUP = ATTN_HEADS // ATTN_KV_HEADS
PAIR_COLS = 2 * ATTN_BLOCK
ATTN_STEP_BLOCKS = 6
assert ATTN_BLOCKS % ATTN_STEP_BLOCKS == 0


def _attend_heads(sink_ref, q_ref, o_ref, query_blocks):
    low_head = _lane_index((ATTN_BLOCK, LANES)) < HEAD_DIM
    zero = jnp.zeros((), BF16)
    odd_cols = lax.broadcasted_iota(jnp.int32, (1, PAIR_COLS), 1) >= ATTN_BLOCK
    tasks = []
    for n, (kv_blocks, masks) in enumerate(query_blocks):
        kv = [kv_blocks(hk) for hk in range(ATTN_KV_HEADS)]
        rows = slice(n * ATTN_BLOCK, (n + 1) * ATTN_BLOCK)
        tasks += [dict(pair=c, rows=rows, kv=kv[(2 * c) // GROUP], masks=masks) for c in range(ATTN_HEADS // 2)]

    def scores_stage(st):
        c = st["pair"]
        tile = q_ref[0, st["rows"], c * LANES:(c + 1) * LANES]
        qs = jnp.concatenate([jnp.where(low_head, tile, zero), jnp.where(low_head, zero, tile)], axis=0)
        scores = [_dot_nt(kb, qs) for kb in st["kv"][0]]
        st["scores"] = [s if mk is None else jnp.where(mk, s, NEG_BIG) for s, mk in zip(scores, st["masks"])]
        st["sink"] = jnp.where(odd_cols, sink_ref[2 * c + 1], sink_ref[2 * c]) * LOG2E

    def max_stage(st):
        m = st["sink"]
        for s in st["scores"]:
            m = jnp.maximum(m, jnp.max(s, axis=0, keepdims=True))
        st["m"] = m

    def values_stage(st):
        denom = jnp.exp2(st["sink"] - st["m"])
        o = None
        for s, vb in zip(st["scores"], st["kv"][1]):
            p = jnp.exp2(s - st["m"])
            denom = denom + jnp.sum(p, axis=0, keepdims=True)
            pv = _dot_tn(vb, p.astype(BF16))
            o = pv if o is None else o + pv
        o = o / denom
        c = st["pair"]
        o_ref[0, st["rows"], c * LANES:(c + 1) * LANES] = jnp.where(
            low_head, o[:, :ATTN_BLOCK].T, o[:, ATTN_BLOCK:].T).astype(BF16)

    stages = (scores_stage, max_stage, values_stage)
    for tick in range(len(tasks) + len(stages) - 1):
        for depth, stage in enumerate(stages):
            if 0 <= tick - depth < len(tasks):
                stage(tasks[tick - depth])


def _attention_kernel(sink_ref, q_ref, k_ref, v_ref, o_ref):
    j2 = lax.broadcasted_iota(jnp.int32, (2 * ATTN_BLOCK, PAIR_COLS), 0)
    t2 = lax.broadcasted_iota(jnp.int32, (2 * ATTN_BLOCK, PAIR_COLS), 1) & (ATTN_BLOCK - 1)
    j1 = lax.broadcasted_iota(jnp.int32, (ATTN_BLOCK, PAIR_COLS), 0)
    t1 = lax.broadcasted_iota(jnp.int32, (ATTN_BLOCK, PAIR_COLS), 1) & (ATTN_BLOCK - 1)

    def query_block(n):
        i = pl.program_id(1) * ATTN_STEP_BLOCKS + n
        prev_start = pl.multiple_of(jnp.maximum(i - 1, 0) * ATTN_BLOCK, ATTN_BLOCK)
        cur_start = pl.multiple_of(i * ATTN_BLOCK, ATTN_BLOCK)
        next_start = pl.multiple_of(jnp.minimum(i + 1, LATENT_ATTN_BLOCKS - 1) * ATTN_BLOCK, ATTN_BLOCK)

        def blocks(ref, hk):
            near = jnp.concatenate([ref[0, hk, pl.ds(prev_start, ATTN_BLOCK), :],
                                    ref[0, hk, pl.ds(cur_start, ATTN_BLOCK), :]], axis=0)
            return near, ref[0, hk, pl.ds(next_start, ATTN_BLOCK), :], ref[0, hk, SEQ:TOKENS, :]

        in_range = jnp.logical_or(j2 >= ATTN_BLOCK, jnp.logical_and(j2 >= t2, i > 0))
        near_valid = jnp.logical_and(in_range, i < LATENT_ATTN_BLOCKS)
        next_valid = jnp.logical_and(j1 <= t1, i < LATENT_ATTN_BLOCKS - 1)
        return (lambda hk: (blocks(k_ref, hk), blocks(v_ref, hk))), (near_valid, next_valid, None)

    _attend_heads(sink_ref, q_ref, o_ref, [query_block(n) for n in range(ATTN_STEP_BLOCKS)])


def _attention(q, k_dup, v_dup, sink):
    bsz = q.shape[0]
    stream = pl.BlockSpec((1, ATTN_KV_HEADS, TOKENS, LANES), lambda b, i: (b, 0, 0, 0))
    step_rows = pl.BlockSpec((1, ATTN_STEP_BLOCKS * ATTN_BLOCK, BRANCH_WIDTH), lambda b, i: (b, i, 0))
    return pl.pallas_call(
        _attention_kernel,
        out_shape=jax.ShapeDtypeStruct((bsz, TOKENS, BRANCH_WIDTH), BF16),
        grid=(bsz, ATTN_BLOCKS // ATTN_STEP_BLOCKS),
        in_specs=[pl.BlockSpec(memory_space=pltpu.SMEM), step_rows, stream, stream],
        out_specs=step_rows,
        compiler_params=_params(("parallel", "arbitrary")),
        name="attention",
    )(sink, q, k_dup, v_dup)


def _cumsum_rows(tri, x):
    hi = x.astype(BF16)
    lo = (x - hi.astype(F32)).astype(BF16)
    return _dot(tri, hi) + _dot(tri, lo)


def _hgrn_kernel(qf_ref, kf_ref, lf_ref, vf_ref, qb_ref, kb_ref, lb_ref, vb_ref, of_ref, ob_ref, state_ref):
    @pl.when(pl.program_id(1) == 0)
    def _():
        state_ref[...] = jnp.zeros(state_ref.shape, F32)

    n = HGRN_CHUNK
    rr = lax.broadcasted_iota(jnp.int32, (n, n), 0)
    cc = lax.broadcasted_iota(jnp.int32, (n, n), 1)
    directions = ((0, cc <= rr, n - 1, n // 2 - 1, qf_ref, kf_ref, lf_ref, vf_ref, of_ref),
                  (1, cc >= rr, 0, n // 2, qb_ref, kb_ref, lb_ref, vb_ref, ob_ref))

    chains = []
    for slot, causal, last_row, mid_row, q_ref, k_ref, lf_ref_d, v_ref, o_ref in directions:
        tri = jnp.where(causal, 1.0, 0.0).astype(BF16)
        for bi in range(HGRN_BATCH):
            g_all = _cumsum_rows(tri, lf_ref_d[bi])
            for hd in range(HGRN_HEADS):
                ln = slice(hd * HGRN_DK, (hd + 1) * HGRN_DK)
                g = g_all[:, ln]
                chains.append(dict(
                    causal=causal, g=g, g_tot=g[last_row:last_row + 1, :], g_mid=g[mid_row:mid_row + 1, :],
                    q=q_ref[bi, :, ln], k=k_ref[bi, :, ln], v=v_ref[bi, :, ln],
                    o_ref=o_ref, state=(bi, slot, hd), out=(bi, slice(None), ln)))

    for ch in chains:
        d = ch["g"] - ch["g_mid"]
        ch["qa"] = ch["q"] * jnp.exp2(jnp.minimum(d, EXP2_CLAMP)).astype(BF16)
        ch["kb"] = ch["k"] * jnp.exp2(jnp.minimum(-d, EXP2_CLAMP)).astype(BF16)
        ch["st_mid"] = state_ref[ch["state"]] * jnp.exp2(ch["g_mid"])
    for ch in chains:
        ch["a"] = _dot_nt(ch["qa"], ch["kb"])
        ch["o_inter"] = _dot_nt(ch["qa"], ch["st_mid"].astype(BF16))
        ch["u"] = _dot_tn(ch["v"], ch["kb"])
    for ch in chains:
        a = jnp.where(ch["causal"], ch["a"], 0.0).astype(BF16)
        ch["o_ref"][ch["out"]] = (ch["o_inter"] + _dot(a, ch["v"])).astype(ch["o_ref"].dtype)
        state_ref[ch["state"]] = (ch["st_mid"] + ch["u"]) * jnp.exp2(ch["g_tot"] - ch["g_mid"])


def _hgrn(hq, hk, hlf, hv):
    bsz = hq.shape[0]
    assert bsz % HGRN_BATCH == 0
    lat = HGRN_CHUNKS - HGRN_CTX_CHUNKS

    def fwd(c):
        return jnp.where(c < HGRN_CTX_CHUNKS, lat + c, c - HGRN_CTX_CHUNKS)

    def bwd(c):
        return HGRN_CHUNKS - 1 - c

    def spec(order, lane_block):
        return pl.BlockSpec((HGRN_BATCH, HGRN_CHUNK, BRANCH_WIDTH), lambda b, c: (b, order(c), lane_block))

    out = jax.ShapeDtypeStruct((bsz, TOKENS, BRANCH_WIDTH), BF16)
    return pl.pallas_call(
        _hgrn_kernel,
        out_shape=(out, out),
        grid=(bsz // HGRN_BATCH, HGRN_CHUNKS),
        in_specs=[spec(fwd, 0), spec(fwd, 0), spec(fwd, 0), spec(fwd, 0),
                  spec(bwd, 0), spec(bwd, 1), spec(bwd, 1), spec(bwd, 0)],
        out_specs=(spec(fwd, 0), spec(bwd, 0)),
        scratch_shapes=[pltpu.VMEM((HGRN_BATCH, 2, HGRN_HEADS, HGRN_DK, HGRN_DK), F32)],
        compiler_params=_params(("parallel", "arbitrary")),
        name="hgrn",
    )(hq, hk, hlf, hv, hq, hk, hlf, hv)


FOURIER_ROW_TILE = 768
FOURIER_TILES = TOKENS // FOURIER_ROW_TILE
FOURIER_LAST_LATENT = SEQ - (FOURIER_TILES - 1) * FOURIER_ROW_TILE
assert FOURIER_TILES * FOURIER_ROW_TILE == TOKENS and FOURIER_LAST_LATENT + CTX_LEN == FOURIER_ROW_TILE


HALF_SEQ = SEQ // 2


def _fourier_kernel(x_ref, xc_ref, wc_ref, cs_ref, csc_ref, o_ref, uv_ref, nyq_ref):
    t = pl.program_id(1)
    last = FOURIER_TILES - 1
    latent_scale = 1.0 / math.sqrt(SEQ * FOURIER_GROUP_DIM)
    blk = LANES

    @pl.when(t == 0)
    def _():
        rr = lax.broadcasted_iota(jnp.int32, (blk, blk), 0)
        cc = lax.broadcasted_iota(jnp.int32, (blk, blk), 1)
        flip = jnp.where(rr + cc == blk, 1.0, 0.0).astype(BF16)
        first_row = lax.broadcasted_iota(jnp.int32, (blk, 1), 0) == 0
        n_blk = HALF_SEQ // blk
        cos_c = wc_ref[:, :BRANCH_WIDTH]
        sin_c = wc_ref[:, BRANCH_WIDTH:]
        for a in range(n_blk):
            src = HALF_SEQ + (n_blk - 1 - a) * blk
            z = _dot(flip, x_ref[0, src:src + blk, :])
            if a > 0:
                z = jnp.where(first_row, x_ref[0, src + blk:src + blk + 16, :].astype(F32)[0:1, :], z)
            lo = x_ref[0, a * blk:(a + 1) * blk, :].astype(F32)
            uv_ref[a * blk:(a + 1) * blk, :] = _dot((lo + z).astype(BF16), cos_c).astype(BF16)
            uv_ref[HALF_SEQ + a * blk:HALF_SEQ + (a + 1) * blk, :] = _dot((lo - z).astype(BF16), sin_c).astype(BF16)
        nyq_ref[...] = _dot(x_ref[0, HALF_SEQ:HALF_SEQ + nyq_ref.shape[0], :], cos_c)

    def latent_rows(r0, n):
        sign = 1.0 - 2.0 * ((r0 + lax.broadcasted_iota(jnp.int32, (n, 1), 0)) & 1).astype(F32)
        return (_dot(cs_ref[pl.ds(r0, n), :], uv_ref[...]) + sign * nyq_ref[0:1, :]) * latent_scale

    @pl.when(t < last)
    def _():
        r0 = pl.multiple_of(t * FOURIER_ROW_TILE, FOURIER_ROW_TILE)
        o_ref[0] = latent_rows(r0, FOURIER_ROW_TILE).astype(BF16)

    @pl.when(t == last)
    def _():
        o_ref[0, 0:FOURIER_LAST_LATENT, :] = latent_rows(last * FOURIER_ROW_TILE, FOURIER_LAST_LATENT).astype(BF16)
        uv = _dot(xc_ref[0], wc_ref[...])
        uvc = jnp.concatenate([uv[:, :BRANCH_WIDTH], uv[:, BRANCH_WIDTH:]], axis=0).astype(BF16)
        yc = _dot(csc_ref[...], uvc)
        o_ref[0, FOURIER_LAST_LATENT:FOURIER_ROW_TILE, :] = (
            yc * (1.0 / math.sqrt(CTX_LEN * FOURIER_GROUP_DIM))).astype(BF16)


def _fourier(four, wc, cs_lat, cs_ctx):
    bsz = four.shape[0]
    return pl.pallas_call(
        _fourier_kernel,
        out_shape=jax.ShapeDtypeStruct((bsz, TOKENS, BRANCH_WIDTH), BF16),
        grid=(bsz, FOURIER_TILES),
        in_specs=[pl.BlockSpec((1, SEQ, BRANCH_WIDTH), lambda b, t: (b, 0, 0)),
                  pl.BlockSpec((1, CTX_LEN, BRANCH_WIDTH), lambda b, t: (b, SEQ // CTX_LEN, 0)),
                  _const_spec(wc.shape), _const_spec(cs_lat.shape), _const_spec(cs_ctx.shape)],
        out_specs=pl.BlockSpec((1, FOURIER_ROW_TILE, BRANCH_WIDTH), lambda b, t: (b, t, 0)),
        scratch_shapes=[pltpu.VMEM((2 * HALF_SEQ, BRANCH_WIDTH), BF16), pltpu.VMEM((16, BRANCH_WIDTH), F32)],
        compiler_params=_params(("parallel", "arbitrary")),
        name="fourier",
    )(four, four, wc, cs_lat, cs_ctx)


def _dft_tables(n, split, n_cols):
    col = jnp.arange(n_cols, dtype=jnp.int32)

    def trig(row_factor):
        ang = ((row_factor[:, None] * col[None, :]) % n).astype(F32) * (2.0 * math.pi / n)
        return jnp.cos(ang), jnp.sin(ang)

    cos_a, sin_a = trig(split * jnp.arange(n // split, dtype=jnp.int32))
    cos_b, sin_b = trig(jnp.arange(split, dtype=jnp.int32))
    cos = cos_a[:, None, :] * cos_b[None, :, :] - sin_a[:, None, :] * sin_b[None, :, :]
    sin = sin_a[:, None, :] * cos_b[None, :, :] + cos_a[:, None, :] * sin_b[None, :, :]
    return jnp.concatenate([cos.reshape(n, n_cols), -sin.reshape(n, n_cols)], axis=1).astype(BF16)


def _channel_dft_table():
    c = jnp.arange(BRANCH_WIDTH, dtype=jnp.int32)
    same = (c[:, None] // FOURIER_GROUP_DIM) == (c[None, :] // FOURIER_GROUP_DIM)
    ang = (((c[:, None] % FOURIER_GROUP_DIM) * (c[None, :] % FOURIER_GROUP_DIM)) % FOURIER_GROUP_DIM
           ).astype(F32) * (2.0 * math.pi / FOURIER_GROUP_DIM)
    cos = jnp.where(same, jnp.cos(ang), 0.0)
    sin = jnp.where(same, jnp.sin(ang), 0.0)
    return jnp.concatenate([cos, sin], axis=1).astype(BF16)


FF_CHUNK = 1024


def _rms(x):
    return x * lax.rsqrt(jnp.mean(x * x, axis=-1, keepdims=True) + EPS)


def _merge_mlp_kernel(x_ref, mod_ref, ctx_mod_ref, of_ref, oa_ref, hf_ref, hb_ref, hg_ref, gate_ref,
                      hng_ref, g2_ref, wb_ref, wo_ref, w1_ref, w2_ref, out_ref):
    d = D_MODEL
    x = x_ref[0]
    mod = _mod_picker(mod_ref, ctx_mod_ref, x.shape[0])

    o_h = hf_ref[0].astype(F32) + hb_ref[0].astype(F32)
    o_n = jnp.concatenate(
        [_rms(o_h[:, hd * HGRN_DK:(hd + 1) * HGRN_DK]) * hng_ref[...] for hd in range(HGRN_HEADS)], axis=-1)
    o_hr = (o_n * _silu(hg_ref[0].astype(F32))).astype(BF16)

    branches = (of_ref[0], oa_ref[0], o_hr)
    mix = None
    for n, br in enumerate(branches):
        term = _sigmoid(gate_ref[0, :, n * d:(n + 1) * d].astype(F32)) * _dot(br, wb_ref[n])
        mix = term if mix is None else mix + term
    y = _dot(mix.astype(BF16), wo_ref[...])
    x1 = x + mod(2) * y

    h2 = (_rms(x1) * (g2_ref[...] * (1.0 + mod(4))) + mod(3)).astype(BF16)
    ff = None
    for c in range(D_FF // FF_CHUNK):
        cols = slice(c * FF_CHUNK, (c + 1) * FF_CHUNK)
        u = jnp.square(jnp.maximum(_dot(h2, w1_ref[:, cols]), 0.0)).astype(BF16)
        part = _dot(u, w2_ref[cols, :])
        ff = part if ff is None else ff + part
    out_ref[0] = x1 + mod(5) * ff


def _merge_mlp(xs, mods, o_four, o_attn, o_hf, o_hb, hg, gate, hng, g2, wb, wo, w1, w2, *, layer, latent_only):
    bsz = xs.shape[0]
    rows = LATENT_ROW_TILE if latent_only else ROW_TILE
    n_rows = SEQ if latent_only else TOKENS
    row = lambda w: pl.BlockSpec((1, rows, w), lambda b, t: (b, t, 0))
    return pl.pallas_call(
        _merge_mlp_kernel,
        out_shape=jax.ShapeDtypeStruct((bsz, n_rows, D_MODEL), F32),
        grid=(bsz, n_rows // rows),
        in_specs=[row(D_MODEL),
                  *_mod_specs(layer),
                  row(BRANCH_WIDTH), row(BRANCH_WIDTH), row(BRANCH_WIDTH), row(BRANCH_WIDTH),
                  row(BRANCH_WIDTH), row(N_BRANCH * D_MODEL),
                  *[_layer_spec(a.shape, layer) for a in (hng, g2, wb, wo, w1, w2)]],
        out_specs=row(D_MODEL),
        compiler_params=_params(("parallel", "parallel")),
        name="merge_mlp",
    )(xs, mods, mods, o_four, o_attn, o_hf, o_hb, hg, gate, hng, g2, wb, wo, w1, w2)


def _rope_tables():
    pos = jnp.arange(SEQ)
    row = (pos // GRID_W).astype(F32)
    col = (pos % GRID_W).astype(F32)
    axis_dim = HEAD_DIM // 2
    half = axis_dim // 2
    inv_freq = ROPE_THETA ** (-jnp.arange(0, axis_dim, 2, dtype=F32) / axis_dim)
    ang_r = row[:, None] * inv_freq
    ang_c = col[:, None] * inv_freq
    ang = jnp.concatenate([ang_r, ang_r, ang_c, ang_c], axis=1)
    sign = jnp.tile(jnp.concatenate([-jnp.ones(half, F32), jnp.ones(half, F32)]), 2)
    cos = jnp.concatenate([jnp.cos(ang), jnp.ones((CTX_LEN, HEAD_DIM), F32)], axis=0)
    sin = jnp.concatenate([jnp.sin(ang) * sign, jnp.zeros((CTX_LEN, HEAD_DIM), F32)], axis=0)
    return jnp.tile(cos, (1, 2)), jnp.tile(sin, (1, 2))


def kernel(x, c, ctx, c_ctx, w_mod, b_mod, norm1_g, norm2_g, w_in, q_norm_g, k_norm_g, attn_sink,
           hgrn_lb_logits, hgrn_norm_g, w_branch, w_out, w_ff1, w_ff2):
    bsz, n_tok, d = x.shape
    depth = w_mod.shape[0]
    assert (n_tok, d, ctx.shape[1]) == (SEQ, D_MODEL, CTX_LEN) and bsz <= MOD_ROWS // 2

    xs = jnp.concatenate([x, ctx], axis=1)
    cond = jnp.zeros((MOD_ROWS, d), F32).at[:bsz].set(c).at[MOD_ROWS // 2].set(c_ctx)
    mods = _modulation(cond, w_mod, b_mod).reshape(depth, MOD_ROWS, 1, 6 * d)

    cos, sin = _rope_tables()
    wc = _channel_dft_table()
    cs_lat = _dft_tables(SEQ, 32, HALF_SEQ)
    cs_ctx = _dft_tables(CTX_LEN, 16, CTX_LEN)
    lb_logits = hgrn_lb_logits.reshape(depth * 2, BRANCH_WIDTH)

    g1 = norm1_g.reshape(depth, 1, d)
    g2 = norm2_g.reshape(depth, 1, d)
    qg = jnp.tile(q_norm_g, (1, 2)).reshape(depth, 1, LANES)
    kg = jnp.tile(k_norm_g, (1, 2)).reshape(depth, 1, LANES)
    hng = hgrn_norm_g.reshape(depth, 1, HGRN_DK)
    w_in, w_branch, w_out, w_ff1, w_ff2 = (w.astype(BF16) for w in (w_in, w_branch, w_out, w_ff1, w_ff2))

    for l in range(depth):
        q, k_dup, v_dup, hv, hk, hlf, hq, hg, four, gate = _inproj(
            xs, mods, g1, w_in, qg, kg, cos, sin, lb_logits, layer=l)
        o_attn = _attention(q, k_dup, v_dup, attn_sink[l])
        o_hf, o_hb = _hgrn(hq, hk, hlf, hv)
        o_four = _fourier(four, wc, cs_lat, cs_ctx)
        xs = _merge_mlp(xs, mods, o_four, o_attn, o_hf, o_hb, hg, gate, hng, g2,
                        w_branch, w_out, w_ff1, w_ff2, layer=l, latent_only=(l == depth - 1))
    return xs
```

```python
import functools
import math

import jax
import jax.numpy as jnp
from jax import lax
from jax.experimental import pallas as pl
from jax.experimental.pallas import tpu as pltpu

F32 = jnp.float32
BF16 = jnp.bfloat16

D_MODEL = 1024
SEQ = 2048
CTX_LEN = 256
TOKENS = SEQ + CTX_LEN
GRID_W = 64
EPS = 1e-6
NEG_BIG = -1e30
LB_FLOOR = 1e-30
ROPE_THETA = 10000.0

BRANCH_WIDTH = D_MODEL // 2
HEAD_DIM = 64
ATTN_HEADS = BRANCH_WIDTH // HEAD_DIM
ATTN_KV_HEADS = ATTN_HEADS // 4
KV_WIDTH = ATTN_KV_HEADS * HEAD_DIM
ATTN_BLOCK = 128
FOURIER_GROUPS = 4
FOURIER_GROUP_DIM = BRANCH_WIDTH // FOURIER_GROUPS
HGRN_HEADS = 4
HGRN_DK = BRANCH_WIDTH // HGRN_HEADS
D_FF = 4 * D_MODEL
N_BRANCH = 3

_SEG = (KV_WIDTH, KV_WIDTH, BRANCH_WIDTH, BRANCH_WIDTH, BRANCH_WIDTH,
        BRANCH_WIDTH, BRANCH_WIDTH, BRANCH_WIDTH, BRANCH_WIDTH, N_BRANCH * D_MODEL)
_OFF = [sum(_SEG[:i]) for i in range(len(_SEG) + 1)]
D_IN = _OFF[-1]

LANES = 128
VMEM_LIMIT_BYTES = 56 * 1024 * 1024

ROW_TILE = 576
ROW_TILES = TOKENS // ROW_TILE
LATENT_ROW_TILE = 512
assert ROW_TILES * ROW_TILE == TOKENS and SEQ % LATENT_ROW_TILE == 0
HGRN_CHUNK = 64
HGRN_CHUNKS = TOKENS // HGRN_CHUNK
HGRN_CTX_CHUNKS = CTX_LEN // HGRN_CHUNK
HGRN_BATCH = 8
EXP2_CLAMP = 100.0
LOG2E = 1.0 / math.log(2.0)
MOD_ROWS = 16


def _const_spec(shape):
    nd = len(shape)
    return pl.BlockSpec(shape, lambda *_: (0,) * nd, pipeline_mode=pl.Buffered(1))


def _layer_spec(stacked_shape, layer):
    rest = tuple(stacked_shape[1:])
    return pl.BlockSpec((None,) + rest, lambda *_: (layer,) + (0,) * len(rest), pipeline_mode=pl.Buffered(1))


def _params(semantics, flags=None):
    return pltpu.CompilerParams(dimension_semantics=semantics, vmem_limit_bytes=VMEM_LIMIT_BYTES,
                                flags=flags)


def _dot(a, b):
    return jnp.dot(a, b, preferred_element_type=F32)


def _dot_nt(a, b):
    return lax.dot_general(a, b, (((1,), (1,)), ((), ())), preferred_element_type=F32)


def _dot_tn(a, b):
    return lax.dot_general(a, b, (((0,), (0,)), ((), ())), preferred_element_type=F32)


def _sigmoid(x):
    return 0.5 * jnp.tanh(0.5 * x) + 0.5


def _silu(x):
    return x * _sigmoid(x)


def _mod_specs(layer):
    one = lambda index: pl.BlockSpec((None, None, 1, 6 * D_MODEL), index)
    return [one(lambda b, t: (layer, b, 0, 0)), one(lambda b, t: (layer, MOD_ROWS // 2, 0, 0))]


def _row_groups(tile_rows):
    split = SEQ % tile_rows or tile_rows // 2
    return slice(0, split), slice(split, tile_rows)


def _group_mods(mod_ref, ctx_mod_ref, tile_rows, rows):
    first_row = pl.program_id(1) * tile_rows + rows.start
    return jnp.where(first_row >= SEQ, ctx_mod_ref[...], mod_ref[...])


MOD_COL_TILE = 1536


def _modulation_kernel(cond_ref, w_ref, b_ref, out_ref):
    a = _silu(cond_ref[...]).astype(BF16)
    out_ref[...] = _dot(a, w_ref[...].astype(BF16)) + b_ref[...]


def _modulation(cond, w_mod, b_mod):
    depth = w_mod.shape[0]
    n = w_mod.shape[2]
    return pl.pallas_call(
        _modulation_kernel,
        out_shape=jax.ShapeDtypeStruct((depth, MOD_ROWS, n), F32),
        grid=(depth, n // MOD_COL_TILE),
        in_specs=[
            pl.BlockSpec((MOD_ROWS, D_MODEL), lambda l, j: (0, 0)),
            pl.BlockSpec((None, D_MODEL, MOD_COL_TILE), lambda l, j: (l, 0, j)),
            pl.BlockSpec((None, 1, MOD_COL_TILE), lambda l, j: (l, 0, j)),
        ],
        out_specs=pl.BlockSpec((None, MOD_ROWS, MOD_COL_TILE), lambda l, j: (l, 0, j)),
        compiler_params=_params(("parallel", "parallel")),
        name="modulation",
    )(cond, w_mod, b_mod.reshape(depth, 1, n))


def _lane_index(shape):
    return lax.broadcasted_iota(jnp.int32, shape, len(shape) - 1)


def _head_mean_matrix():
    shift = HEAD_DIM.bit_length() - 1
    r = lax.broadcasted_iota(jnp.int32, (LANES, LANES), 0) >> shift
    c = lax.broadcasted_iota(jnp.int32, (LANES, LANES), 1) >> shift
    return jnp.where(r == c, 1.0 / HEAD_DIM, 0.0).astype(BF16)


def _head_rms_norm(u, gain, mean_bd):
    ms = _dot((u * u).astype(BF16), mean_bd)
    return u * lax.rsqrt(ms + EPS) * gain


def _rope(u, cos, sin_signed):
    first = (_lane_index(u.shape) & 31) < 16
    partner = jnp.where(first, pltpu.roll(u, LANES - 16, axis=1), pltpu.roll(u, 16, axis=1))
    return u * cos + partner * sin_signed


def _hgrn_lower_bound(lb_ref, layer, direction):
    depth = lb_ref.shape[0] // 2
    rows = [lb_ref[2 * j + direction:2 * j + direction + 1, :] for j in range(depth)]
    m = functools.reduce(jnp.maximum, rows)
    e = [jnp.exp(r - m) for r in rows]
    z = functools.reduce(lambda a, b: a + b, e)
    p = [x / z for x in e]
    csum = functools.reduce(lambda a, b: a + b, p[:layer + 1])
    return csum - p[0]


def _hgrn_forget(z, lb):
    t = jnp.exp(-jnp.abs(z))
    pos = z >= 0.0
    lb_floor = jnp.maximum(lb, LB_FLOOR)
    r = 1.0 / (1.0 + t)
    tr = t * r
    k = (1.0 - lb) * jnp.where(pos, tr, r)
    f = lb_floor + (1.0 - lb_floor) * jnp.where(pos, r, tr)
    return k, jnp.log(f) * LOG2E


GATE_CHUNKS = 6


def _inproj_kernel(x_ref, mod_ref, ctx_mod_ref, g1_ref, w_ref, qg_ref, kg_ref, cos_ref, sin_ref, lb_ref,
                   q_ref, k_ref, v_ref, hv_ref, hk_ref, hlf_ref, hq_ref, hg_ref, four_ref, gate_ref,
                   *, layer):
    tile_rows = x_ref.shape[1]
    groups = [dict(rows=rows) for rows in _row_groups(tile_rows)]
    mean_bd = _head_mean_matrix()
    q_gain = qg_ref[...] * (HEAD_DIM ** -0.5 * LOG2E)
    gate_width = N_BRANCH * D_MODEL // GATE_CHUNKS

    for st in groups:
        rows = st["rows"]
        mod = _group_mods(mod_ref, ctx_mod_ref, tile_rows, rows)
        x = x_ref[0, rows, :]
        ms = jnp.mean(x * x, axis=-1, keepdims=True)
        row_gain = g1_ref[...] * (1.0 + mod[:, D_MODEL:2 * D_MODEL])
        st["h"] = (x * lax.rsqrt(ms + EPS) * row_gain + mod[:, 0:D_MODEL]).astype(BF16)
        st["cos"] = cos_ref[rows, :]
        st["sin"] = sin_ref[rows, :]
        st["low_head"] = _lane_index((rows.stop - rows.start, LANES)) < HEAD_DIM

    def proj(st, seg, width=None):
        a = _OFF[seg]
        b = _OFF[seg + 1] if width is None else a + width
        return _dot(st["h"], w_ref[:, a:b])

    def gate_chunk(st, i):
        a = _OFF[9] + i * gate_width
        gate_ref[0, st["rows"], i * gate_width:(i + 1) * gate_width] = _dot(
            st["h"], w_ref[:, a:a + gate_width]).astype(BF16)

    for st in groups:
        st["kv"] = proj(st, 0, 2 * KV_WIDTH)
        gate_chunk(st, 0)
    for st in groups:
        rows, low_head, kv = st["rows"], st["low_head"], st.pop("kv")
        kr = _rope(_head_rms_norm(kv[:, :KV_WIDTH], kg_ref[...], mean_bd), st["cos"], st["sin"])
        kr_sw = pltpu.roll(kr, HEAD_DIM, axis=1)
        k_ref[0, 0, rows, :] = jnp.where(low_head, kr, kr_sw).astype(BF16)
        k_ref[0, 1, rows, :] = jnp.where(low_head, kr_sw, kr).astype(BF16)
        vr = kv[:, KV_WIDTH:]
        vr_sw = pltpu.roll(vr, HEAD_DIM, axis=1)
        v_ref[0, 0, rows, :] = jnp.where(low_head, vr, vr_sw).astype(BF16)
        v_ref[0, 1, rows, :] = jnp.where(low_head, vr_sw, vr).astype(BF16)

    for st in groups:
        st["q"] = proj(st, 5)
        gate_chunk(st, 1)
    for st in groups:
        qr = st.pop("q")
        for c in range(BRANCH_WIDTH // LANES):
            u = qr[:, c * LANES:(c + 1) * LANES]
            q_ref[0, st["rows"], c * LANES:(c + 1) * LANES] = _rope(
                _head_rms_norm(u, q_gain, mean_bd), st["cos"], st["sin"]).astype(BF16)

    for direction, seg in ((0, 3), (1, 4)):
        lb = _hgrn_lower_bound(lb_ref, layer, direction)
        cols = slice(direction * BRANCH_WIDTH, (direction + 1) * BRANCH_WIDTH)
        for st in groups:
            st["z"] = proj(st, seg)
            gate_chunk(st, 2 + direction)
        for st in groups:
            k, log2f = _hgrn_forget(st.pop("z"), lb)
            hk_ref[0, st["rows"], cols] = k.astype(BF16)
            hlf_ref[0, st["rows"], cols] = log2f
    for st in groups:
        st["qh"] = proj(st, 6)
        gate_chunk(st, 4)
    for st in groups:
        hq_ref[0, st["rows"], :] = _silu(st.pop("qh")).astype(BF16)
    for st in groups:
        rows = st["rows"]
        hv_ref[0, rows, :] = proj(st, 2).astype(BF16)
        hg_ref[0, rows, :] = proj(st, 7).astype(BF16)
        four_ref[0, rows, :] = proj(st, 8).astype(BF16)
        gate_chunk(st, 5)


def _inproj(xs, mods, g1, w_in, qg, kg, cos, sin, lb_logits, *, layer):
    bsz = xs.shape[0]
    row = lambda w: pl.BlockSpec((1, ROW_TILE, w), lambda b, t: (b, t, 0))
    dup = pl.BlockSpec((1, ATTN_KV_HEADS, ROW_TILE, LANES), lambda b, t: (b, 0, t, 0))
    tok = lambda w, dt: jax.ShapeDtypeStruct((bsz, TOKENS, w), dt)
    dup_shape = jax.ShapeDtypeStruct((bsz, ATTN_KV_HEADS, TOKENS, LANES), BF16)
    return pl.pallas_call(
        functools.partial(_inproj_kernel, layer=layer),
        out_shape=(tok(BRANCH_WIDTH, BF16), dup_shape, dup_shape,
                   tok(BRANCH_WIDTH, BF16), tok(2 * BRANCH_WIDTH, BF16), tok(2 * BRANCH_WIDTH, F32),
                   tok(BRANCH_WIDTH, BF16), tok(BRANCH_WIDTH, BF16), tok(BRANCH_WIDTH, BF16),
                   tok(N_BRANCH * D_MODEL, BF16)),
        grid=(bsz, ROW_TILES),
        in_specs=[
            row(D_MODEL),
            *_mod_specs(layer),
            _layer_spec(g1.shape, layer),
            _layer_spec(w_in.shape, layer),
            _layer_spec(qg.shape, layer),
            _layer_spec(kg.shape, layer),
            pl.BlockSpec((ROW_TILE, LANES), lambda b, t: (t, 0)),
            pl.BlockSpec((ROW_TILE, LANES), lambda b, t: (t, 0)),
            _const_spec(lb_logits.shape),
        ],
        out_specs=(row(BRANCH_WIDTH), dup, dup,
                   row(BRANCH_WIDTH), row(2 * BRANCH_WIDTH), row(2 * BRANCH_WIDTH),
                   row(BRANCH_WIDTH), row(BRANCH_WIDTH), row(BRANCH_WIDTH),
                   row(N_BRANCH * D_MODEL)),
        compiler_params=_params(("parallel", "parallel")),
        name="inproj",
    )(xs, mods, mods, g1, w_in, qg, kg, cos, sin, lb_logits)


ATTN_BLOCKS = TOKENS // ATTN_BLOCK
LATENT_ATTN_BLOCKS = SEQ // ATTN_BLOCK
GROUP = ATTN_HEADS // ATTN_KV_HEADS
PAIR_COLS = 2 * ATTN_BLOCK
ATTN_STEP_BLOCKS = 6
assert ATTN_BLOCKS % ATTN_STEP_BLOCKS == 0


def _attend_heads(sink_ref, q_ref, o_ref, query_blocks):
    low_head = _lane_index((ATTN_BLOCK, LANES)) < HEAD_DIM
    zero = jnp.zeros((), BF16)
    odd_cols = lax.broadcasted_iota(jnp.int32, (1, PAIR_COLS), 1) >= ATTN_BLOCK
    tasks = []
    for n, (kv_blocks, masks) in enumerate(query_blocks):
        kv = [kv_blocks(hk) for hk in range(ATTN_KV_HEADS)]
        rows = slice(n * ATTN_BLOCK, (n + 1) * ATTN_BLOCK)
        tasks += [dict(pair=c, rows=rows, kv=kv[(2 * c) // GROUP], masks=masks) for c in range(ATTN_HEADS // 2)]

    def scores_stage(st):
        c = st["pair"]
        tile = q_ref[0, st["rows"], c * LANES:(c + 1) * LANES]
        qs = jnp.concatenate([jnp.where(low_head, tile, zero), jnp.where(low_head, zero, tile)], axis=0)
        scores = [_dot_nt(kb, qs) for kb in st["kv"][0]]
        st["scores"] = [s if mk is None else jnp.where(mk, s, NEG_BIG) for s, mk in zip(scores, st["masks"])]
        st["sink"] = jnp.where(odd_cols, sink_ref[2 * c + 1], sink_ref[2 * c]) * LOG2E

    def max_stage(st):
        m = st["sink"]
        for s in st["scores"]:
            m = jnp.maximum(m, jnp.max(s, axis=0, keepdims=True))
        st["m"] = m

    def values_stage(st):
        denom = jnp.exp2(st["sink"] - st["m"])
        o = None
        for s, vb in zip(st["scores"], st["kv"][1]):
            p = jnp.exp2(s - st["m"])
            denom = denom + jnp.sum(p, axis=0, keepdims=True)
            pv = _dot_tn(vb, p.astype(BF16))
            o = pv if o is None else o + pv
        o = o / denom
        c = st["pair"]
        o_ref[0, st["rows"], c * LANES:(c + 1) * LANES] = jnp.where(
            low_head, o[:, :ATTN_BLOCK].T, o[:, ATTN_BLOCK:].T).astype(BF16)

    stages = (scores_stage, max_stage, values_stage)
    for tick in range(len(tasks) + len(stages) - 1):
        for depth, stage in enumerate(stages):
            if 0 <= tick - depth < len(tasks):
                stage(tasks[tick - depth])


def _attention_kernel(sink_ref, q_ref, k_ref, v_ref, o_ref):
    j2 = lax.broadcasted_iota(jnp.int32, (2 * ATTN_BLOCK, PAIR_COLS), 0)
    t2 = lax.broadcasted_iota(jnp.int32, (2 * ATTN_BLOCK, PAIR_COLS), 1) & (ATTN_BLOCK - 1)
    j1 = lax.broadcasted_iota(jnp.int32, (ATTN_BLOCK, PAIR_COLS), 0)
    t1 = lax.broadcasted_iota(jnp.int32, (ATTN_BLOCK, PAIR_COLS), 1) & (ATTN_BLOCK - 1)

    def query_block(n):
        i = pl.program_id(1) * ATTN_STEP_BLOCKS + n
        prev_start = pl.multiple_of(jnp.maximum(i - 1, 0) * ATTN_BLOCK, ATTN_BLOCK)
        cur_start = pl.multiple_of(i * ATTN_BLOCK, ATTN_BLOCK)
        next_start = pl.multiple_of(jnp.minimum(i + 1, LATENT_ATTN_BLOCKS - 1) * ATTN_BLOCK, ATTN_BLOCK)

        def blocks(ref, hk):
            near = jnp.concatenate([ref[0, hk, pl.ds(prev_start, ATTN_BLOCK), :],
                                    ref[0, hk, pl.ds(cur_start, ATTN_BLOCK), :]], axis=0)
            return near, ref[0, hk, pl.ds(next_start, ATTN_BLOCK), :], ref[0, hk, SEQ:TOKENS, :]

        in_range = jnp.logical_or(j2 >= ATTN_BLOCK, jnp.logical_and(j2 >= t2, i > 0))
        near_valid = jnp.logical_and(in_range, i < LATENT_ATTN_BLOCKS)
        next_valid = jnp.logical_and(j1 <= t1, i < LATENT_ATTN_BLOCKS - 1)
        return (lambda hk: (blocks(k_ref, hk), blocks(v_ref, hk))), (near_valid, next_valid, None)

    _attend_heads(sink_ref, q_ref, o_ref, [query_block(n) for n in range(ATTN_STEP_BLOCKS)])


def _attention(q, k_dup, v_dup, sink):
    bsz = q.shape[0]
    stream = pl.BlockSpec((1, ATTN_KV_HEADS, TOKENS, LANES), lambda b, i: (b, 0, 0, 0))
    step_rows = pl.BlockSpec((1, ATTN_STEP_BLOCKS * ATTN_BLOCK, BRANCH_WIDTH), lambda b, i: (b, i, 0))
    return pl.pallas_call(
        _attention_kernel,
        out_shape=jax.ShapeDtypeStruct((bsz, TOKENS, BRANCH_WIDTH), BF16),
        grid=(bsz, ATTN_BLOCKS // ATTN_STEP_BLOCKS),
        in_specs=[pl.BlockSpec(memory_space=pltpu.SMEM), step_rows, stream, stream],
        out_specs=step_rows,
        compiler_params=_params(("parallel", "arbitrary")),
        name="attention",
    )(sink, q, k_dup, v_dup)


def _cumsum_rows(tri, x):
    hi = x.astype(BF16)
    lo = (x - hi.astype(F32)).astype(BF16)
    return _dot(tri, hi) + _dot(tri, lo)


def _hgrn_kernel(qf_ref, kf_ref, lf_ref, vf_ref, qb_ref, kb_ref, lb_ref, vb_ref, of_ref, ob_ref, state_ref):
    @pl.when(pl.program_id(1) == 0)
    def _():
        state_ref[...] = jnp.zeros(state_ref.shape, F32)

    n = HGRN_CHUNK
    rr = lax.broadcasted_iota(jnp.int32, (n, n), 0)
    cc = lax.broadcasted_iota(jnp.int32, (n, n), 1)
    directions = ((0, cc <= rr, n - 1, n // 2 - 1, qf_ref, kf_ref, lf_ref, vf_ref, of_ref),
                  (1, cc >= rr, 0, n // 2, qb_ref, kb_ref, lb_ref, vb_ref, ob_ref))

    chains = []
    for slot, causal, last_row, mid_row, q_ref, k_ref, lf_ref_d, v_ref, o_ref in directions:
        tri = jnp.where(causal, 1.0, 0.0).astype(BF16)
        for bi in range(HGRN_BATCH):
            g_all = _cumsum_rows(tri, lf_ref_d[bi])
            for hd in range(HGRN_HEADS):
                ln = slice(hd * HGRN_DK, (hd + 1) * HGRN_DK)
                g = g_all[:, ln]
                chains.append(dict(
                    causal=causal, g=g, g_tot=g[last_row:last_row + 1, :], g_mid=g[mid_row:mid_row + 1, :],
                    q=q_ref[bi, :, ln], k=k_ref[bi, :, ln], v=v_ref[bi, :, ln],
                    o_ref=o_ref, state=(bi, slot, hd), out=(bi, slice(None), ln)))

    for ch in chains:
        d = ch["g"] - ch["g_mid"]
        ch["qa"] = ch["q"] * jnp.exp2(jnp.minimum(d, EXP2_CLAMP)).astype(BF16)
        ch["kb"] = ch["k"] * jnp.exp2(jnp.minimum(-d, EXP2_CLAMP)).astype(BF16)
        ch["st_mid"] = state_ref[ch["state"]] * jnp.exp2(ch["g_mid"])
    for ch in chains:
        ch["a"] = _dot_nt(ch["qa"], ch["kb"])
        ch["o_inter"] = _dot_nt(ch["qa"], ch["st_mid"].astype(BF16))
        ch["u"] = _dot_tn(ch["v"], ch["kb"])
    for ch in chains:
        a = jnp.where(ch["causal"], ch["a"], 0.0).astype(BF16)
        ch["o_ref"][ch["out"]] = (ch["o_inter"] + _dot(a, ch["v"])).astype(ch["o_ref"].dtype)
        state_ref[ch["state"]] = (ch["st_mid"] + ch["u"]) * jnp.exp2(ch["g_tot"] - ch["g_mid"])


def _hgrn(hq, hk, hlf, hv):
    bsz = hq.shape[0]
    assert bsz % HGRN_BATCH == 0
    lat = HGRN_CHUNKS - HGRN_CTX_CHUNKS

    def fwd(c):
        return jnp.where(c < HGRN_CTX_CHUNKS, lat + c, c - HGRN_CTX_CHUNKS)

    def bwd(c):
        return HGRN_CHUNKS - 1 - c

    def spec(order, lane_block):
        return pl.BlockSpec((HGRN_BATCH, HGRN_CHUNK, BRANCH_WIDTH), lambda b, c: (b, order(c), lane_block))

    out = jax.ShapeDtypeStruct((bsz, TOKENS, BRANCH_WIDTH), BF16)
    return pl.pallas_call(
        _hgrn_kernel,
        out_shape=(out, out),
        grid=(bsz // HGRN_BATCH, HGRN_CHUNKS),
        in_specs=[spec(fwd, 0), spec(fwd, 0), spec(fwd, 0), spec(fwd, 0),
                  spec(bwd, 0), spec(bwd, 1), spec(bwd, 1), spec(bwd, 0)],
        out_specs=(spec(fwd, 0), spec(bwd, 0)),
        scratch_shapes=[pltpu.VMEM((HGRN_BATCH, 2, HGRN_HEADS, HGRN_DK, HGRN_DK), F32)],
        compiler_params=_params(("parallel", "arbitrary")),
        name="hgrn",
    )(hq, hk, hlf, hv, hq, hk, hlf, hv)


FOURIER_ROW_TILE = 768
FOURIER_TILES = TOKENS // FOURIER_ROW_TILE
FOURIER_LAST_LATENT = SEQ - (FOURIER_TILES - 1) * FOURIER_ROW_TILE
assert FOURIER_TILES * FOURIER_ROW_TILE == TOKENS and FOURIER_LAST_LATENT + CTX_LEN == FOURIER_ROW_TILE


HALF_SEQ = SEQ // 2


def _fourier_kernel(x_ref, xc_ref, wc_ref, cs_ref, csc_ref, o_ref, uv_ref, nyq_ref):
    t = pl.program_id(1)
    last = FOURIER_TILES - 1
    latent_scale = 1.0 / math.sqrt(SEQ * FOURIER_GROUP_DIM)
    blk = LANES

    @pl.when(t == 0)
    def _():
        rr = lax.broadcasted_iota(jnp.int32, (blk, blk), 0)
        cc = lax.broadcasted_iota(jnp.int32, (blk, blk), 1)
        flip = jnp.where(rr + cc == blk, 1.0, 0.0).astype(BF16)
        first_row = lax.broadcasted_iota(jnp.int32, (blk, 1), 0) == 0
        n_blk = HALF_SEQ // blk
        cos_c = wc_ref[:, :BRANCH_WIDTH]
        sin_c = wc_ref[:, BRANCH_WIDTH:]
        for a in range(n_blk):
            src = HALF_SEQ + (n_blk - 1 - a) * blk
            z = _dot(flip, x_ref[0, src:src + blk, :])
            if a > 0:
                z = jnp.where(first_row, x_ref[0, src + blk:src + blk + 16, :].astype(F32)[0:1, :], z)
            lo = x_ref[0, a * blk:(a + 1) * blk, :].astype(F32)
            uv_ref[a * blk:(a + 1) * blk, :] = _dot((lo + z).astype(BF16), cos_c).astype(BF16)
            uv_ref[HALF_SEQ + a * blk:HALF_SEQ + (a + 1) * blk, :] = _dot((lo - z).astype(BF16), sin_c).astype(BF16)
        nyq_ref[...] = _dot(x_ref[0, HALF_SEQ:HALF_SEQ + nyq_ref.shape[0], :], cos_c)

    def latent_rows(r0, n):
        sign = 1.0 - 2.0 * ((r0 + lax.broadcasted_iota(jnp.int32, (n, 1), 0)) & 1).astype(F32)
        return (_dot(cs_ref[pl.ds(r0, n), :], uv_ref[...]) + sign * nyq_ref[0:1, :]) * latent_scale

    @pl.when(t < last)
    def _():
        r0 = pl.multiple_of(t * FOURIER_ROW_TILE, FOURIER_ROW_TILE)
        o_ref[0] = latent_rows(r0, FOURIER_ROW_TILE).astype(BF16)

    @pl.when(t == last)
    def _():
        o_ref[0, 0:FOURIER_LAST_LATENT, :] = latent_rows(last * FOURIER_ROW_TILE, FOURIER_LAST_LATENT).astype(BF16)
        uv = _dot(xc_ref[0], wc_ref[...])
        uvc = jnp.concatenate([uv[:, :BRANCH_WIDTH], uv[:, BRANCH_WIDTH:]], axis=0).astype(BF16)
        yc = _dot(csc_ref[...], uvc)
        o_ref[0, FOURIER_LAST_LATENT:FOURIER_ROW_TILE, :] = (
            yc * (1.0 / math.sqrt(CTX_LEN * FOURIER_GROUP_DIM))).astype(BF16)


def _fourier(four, wc, cs_lat, cs_ctx):
    bsz = four.shape[0]
    return pl.pallas_call(
        _fourier_kernel,
        out_shape=jax.ShapeDtypeStruct((bsz, TOKENS, BRANCH_WIDTH), BF16),
        grid=(bsz, FOURIER_TILES),
        in_specs=[pl.BlockSpec((1, SEQ, BRANCH_WIDTH), lambda b, t: (b, 0, 0)),
                  pl.BlockSpec((1, CTX_LEN, BRANCH_WIDTH), lambda b, t: (b, SEQ // CTX_LEN, 0)),
                  _const_spec(wc.shape), _const_spec(cs_lat.shape), _const_spec(cs_ctx.shape)],
        out_specs=pl.BlockSpec((1, FOURIER_ROW_TILE, BRANCH_WIDTH), lambda b, t: (b, t, 0)),
        scratch_shapes=[pltpu.VMEM((2 * HALF_SEQ, BRANCH_WIDTH), BF16), pltpu.VMEM((16, BRANCH_WIDTH), F32)],
        compiler_params=_params(("parallel", "arbitrary")),
        name="fourier",
    )(four, four, wc, cs_lat, cs_ctx)


def _dft_tables(n, split, n_cols):
    col = jnp.arange(n_cols, dtype=jnp.int32)

    def trig(row_factor):
        ang = ((row_factor[:, None] * col[None, :]) % n).astype(F32) * (2.0 * math.pi / n)
        return jnp.cos(ang), jnp.sin(ang)

    cos_a, sin_a = trig(split * jnp.arange(n // split, dtype=jnp.int32))
    cos_b, sin_b = trig(jnp.arange(split, dtype=jnp.int32))
    cos = cos_a[:, None, :] * cos_b[None, :, :] - sin_a[:, None, :] * sin_b[None, :, :]
    sin = sin_a[:, None, :] * cos_b[None, :, :] + cos_a[:, None, :] * sin_b[None, :, :]
    return jnp.concatenate([cos.reshape(n, n_cols), -sin.reshape(n, n_cols)], axis=1).astype(BF16)


def _channel_dft_table():
    c = jnp.arange(BRANCH_WIDTH, dtype=jnp.int32)
    same = (c[:, None] // FOURIER_GROUP_DIM) == (c[None, :] // FOURIER_GROUP_DIM)
    ang = (((c[:, None] % FOURIER_GROUP_DIM) * (c[None, :] % FOURIER_GROUP_DIM)) % FOURIER_GROUP_DIM
           ).astype(F32) * (2.0 * math.pi / FOURIER_GROUP_DIM)
    cos = jnp.where(same, jnp.cos(ang), 0.0)
    sin = jnp.where(same, jnp.sin(ang), 0.0)
    return jnp.concatenate([cos, sin], axis=1).astype(BF16)


FF_CHUNK = 1024


def _rms(x):
    return x * lax.rsqrt(jnp.mean(x * x, axis=-1, keepdims=True) + EPS)


def _merge_mlp_kernel(x_ref, mod_ref, ctx_mod_ref, of_ref, oa_ref, hf_ref, hb_ref, hg_ref, gate_ref,
                      hng_ref, g2_ref, wb_ref, wo_ref, w1_ref, w2_ref, out_ref):
    d = D_MODEL
    tile_rows = x_ref.shape[1]
    groups = [dict(rows=rows) for rows in _row_groups(tile_rows)]

    def mix_stage(st):
        rows = st["rows"]
        mods = _group_mods(mod_ref, ctx_mod_ref, tile_rows, rows)
        mod = st["mod"] = lambda i: mods[:, i * d:(i + 1) * d]
        o_h = hf_ref[0, rows, :].astype(F32) + hb_ref[0, rows, :].astype(F32)
        o_n = jnp.concatenate(
            [_rms(o_h[:, hd * HGRN_DK:(hd + 1) * HGRN_DK]) * hng_ref[...] for hd in range(HGRN_HEADS)], axis=-1)
        o_hr = (o_n * _silu(hg_ref[0, rows, :].astype(F32))).astype(BF16)
        branches = (of_ref[0, rows, :], oa_ref[0, rows, :], o_hr)
        mix = None
        for n, br in enumerate(branches):
            term = _sigmoid(gate_ref[0, rows, n * d:(n + 1) * d].astype(F32)) * _dot(br, wb_ref[n])
            mix = term if mix is None else mix + term
        y = _dot(mix.astype(BF16), wo_ref[...])
        st["x1"] = x1 = x_ref[0, rows, :] + mod(2) * y
        st["h2"] = (_rms(x1) * (g2_ref[...] * (1.0 + mod(4))) + mod(3)).astype(BF16)

    def mlp_stage(st):
        ff = None
        for c in range(D_FF // FF_CHUNK):
            cols = slice(c * FF_CHUNK, (c + 1) * FF_CHUNK)
            u = jnp.square(jnp.maximum(_dot(st["h2"], w1_ref[:, cols]), 0.0)).astype(BF16)
            part = _dot(u, w2_ref[cols, :])
            ff = part if ff is None else ff + part
        out_ref[0, st["rows"], :] = st["x1"] + st["mod"](5) * ff

    for stage in (mix_stage, mlp_stage):
        for st in groups:
            stage(st)


def _merge_mlp(xs, mods, o_four, o_attn, o_hf, o_hb, hg, gate, hng, g2, wb, wo, w1, w2, *, layer, latent_only):
    bsz = xs.shape[0]
    rows = LATENT_ROW_TILE if latent_only else ROW_TILE
    n_rows = SEQ if latent_only else TOKENS
    row = lambda w: pl.BlockSpec((1, rows, w), lambda b, t: (b, t, 0))
    return pl.pallas_call(
        _merge_mlp_kernel,
        out_shape=jax.ShapeDtypeStruct((bsz, n_rows, D_MODEL), F32),
        grid=(bsz, n_rows // rows),
        in_specs=[row(D_MODEL),
                  *_mod_specs(layer),
                  row(BRANCH_WIDTH), row(BRANCH_WIDTH), row(BRANCH_WIDTH), row(BRANCH_WIDTH),
                  row(BRANCH_WIDTH), row(N_BRANCH * D_MODEL),
                  *[_layer_spec(a.shape, layer) for a in (hng, g2, wb, wo, w1, w2)]],
        out_specs=row(D_MODEL),
        compiler_params=_params(("parallel", "parallel")),
        name="merge_mlp",
    )(xs, mods, mods, o_four, o_attn, o_hf, o_hb, hg, gate, hng, g2, wb, wo, w1, w2)


def _rope_tables():
    pos = jnp.arange(SEQ)
    row = (pos // GRID_W).astype(F32)
    col = (pos % GRID_W).astype(F32)
    axis_dim = HEAD_DIM // 2
    half = axis_dim // 2
    inv_freq = ROPE_THETA ** (-jnp.arange(0, axis_dim, 2, dtype=F32) / axis_dim)
    ang_r = row[:, None] * inv_freq
    ang_c = col[:, None] * inv_freq
    ang = jnp.concatenate([ang_r, ang_r, ang_c, ang_c], axis=1)
    sign = jnp.tile(jnp.concatenate([-jnp.ones(half, F32), jnp.ones(half, F32)]), 2)
    cos = jnp.concatenate([jnp.cos(ang), jnp.ones((CTX_LEN, HEAD_DIM), F32)], axis=0)
    sin = jnp.concatenate([jnp.sin(ang) * sign, jnp.zeros((CTX_LEN, HEAD_DIM), F32)], axis=0)
    return jnp.tile(cos, (1, 2)), jnp.tile(sin, (1, 2))


def kernel(x, c, ctx, c_ctx, w_mod, b_mod, norm1_g, norm2_g, w_in, q_norm_g, k_norm_g, attn_sink,
           hgrn_lb_logits, hgrn_norm_g, w_branch, w_out, w_ff1, w_ff2):
    bsz, n_tok, d = x.shape
    depth = w_mod.shape[0]
    assert (n_tok, d, ctx.shape[1]) == (SEQ, D_MODEL, CTX_LEN) and bsz <= MOD_ROWS // 2

    xs = jnp.concatenate([x, ctx], axis=1)
    cond = jnp.zeros((MOD_ROWS, d), F32).at[:bsz].set(c).at[MOD_ROWS // 2].set(c_ctx)
    mods = _modulation(cond, w_mod, b_mod).reshape(depth, MOD_ROWS, 1, 6 * d)

    cos, sin = _rope_tables()
    wc = _channel_dft_table()
    cs_lat = _dft_tables(SEQ, 32, HALF_SEQ)
    cs_ctx = _dft_tables(CTX_LEN, 16, CTX_LEN)
    lb_logits = hgrn_lb_logits.reshape(depth * 2, BRANCH_WIDTH)

    g1 = norm1_g.reshape(depth, 1, d)
    g2 = norm2_g.reshape(depth, 1, d)
    qg = jnp.tile(q_norm_g, (1, 2)).reshape(depth, 1, LANES)
    kg = jnp.tile(k_norm_g, (1, 2)).reshape(depth, 1, LANES)
    hng = hgrn_norm_g.reshape(depth, 1, HGRN_DK)
    w_in, w_branch, w_out, w_ff1, w_ff2 = (w.astype(BF16) for w in (w_in, w_branch, w_out, w_ff1, w_ff2))

    for l in range(depth):
        q, k_dup, v_dup, hv, hk, hlf, hq, hg, four, gate = _inproj(
            xs, mods, g1, w_in, qg, kg, cos, sin, lb_logits, layer=l)
        o_attn = _attention(q, k_dup, v_dup, attn_sink[l])
        o_hf, o_hb = _hgrn(hq, hk, hlf, hv)
        o_four = _fourier(four, wc, cs_lat, cs_ctx)
        xs = _merge_mlp(xs, mods, o_four, o_attn, o_hf, o_hb, hg, gate, hng, g2,
                        w_branch, w_out, w_ff1, w_ff2, layer=l, latent_only=(l == depth - 1))
    return xs
```

```python
import functools
import math

import jax
import jax.numpy as jnp
from jax import lax
from jax.experimental import pallas as pl
from jax.experimental.pallas import tpu as pltpu

F32 = jnp.float32
BF16 = jnp.bfloat16

D_MODEL = 1024
SEQ = 2048
CTX_LEN = 256
TOKENS = SEQ + CTX_LEN
GRID_W = 64
EPS = 1e-6
NEG_BIG = -1e30
LB_FLOOR = 1e-30
ROPE_THETA = 10000.0

BRANCH_WIDTH = D_MODEL // 2
HEAD_DIM = 64
ATTN_HEADS = BRANCH_WIDTH // HEAD_DIM
ATTN_KV_HEADS = ATTN_HEADS // 4
KV_WIDTH = ATTN_KV_HEADS * HEAD_DIM
ATTN_BLOCK = 128
FOURIER_GROUPS = 4
FOURIER_GROUP_DIM = BRANCH_WIDTH // FOURIER_GROUPS
HGRN_HEADS = 4
HGRN_DK = BRANCH_WIDTH // HGRN_HEADS
D_FF = 4 * D_MODEL
N_BRANCH = 3

_SEG = (KV_WIDTH, KV_WIDTH, BRANCH_WIDTH, BRANCH_WIDTH, BRANCH_WIDTH,
        BRANCH_WIDTH, BRANCH_WIDTH, BRANCH_WIDTH, BRANCH_WIDTH, N_BRANCH * D_MODEL)
_OFF = [sum(_SEG[:i]) for i in range(len(_SEG) + 1)]
D_IN = _OFF[-1]

LANES = 128
VMEM_LIMIT_BYTES = 56 * 1024 * 1024

ROW_TILE = 576
ROW_TILES = TOKENS // ROW_TILE
LATENT_ROW_TILE = 512
assert ROW_TILES * ROW_TILE == TOKENS and SEQ % LATENT_ROW_TILE == 0
HGRN_CHUNK = 64
HGRN_BATCH = 8
EXP2_CLAMP = 100.0
LOG2E = 1.0 / math.log(2.0)
MOD_ROWS = 16


def _const_spec(shape):
    nd = len(shape)
    return pl.BlockSpec(shape, lambda *_: (0,) * nd, pipeline_mode=pl.Buffered(1))


def _layer_spec(stacked_shape, layer):
    rest = tuple(stacked_shape[1:])
    return pl.BlockSpec((None,) + rest, lambda *_: (layer,) + (0,) * len(rest), pipeline_mode=pl.Buffered(1))


def _params(semantics, flags=None):
    return pltpu.CompilerParams(dimension_semantics=semantics, vmem_limit_bytes=VMEM_LIMIT_BYTES,
                                flags=flags)


def _dot(a, b):
    return jnp.dot(a, b, preferred_element_type=F32)


def _dot_nt(a, b):
    return lax.dot_general(a, b, (((1,), (1,)), ((), ())), preferred_element_type=F32)


def _dot_tn(a, b):
    return lax.dot_general(a, b, (((0,), (0,)), ((), ())), preferred_element_type=F32)


def _sigmoid(x):
    return 0.5 * jnp.tanh(0.5 * x) + 0.5


def _silu(x):
    return x * _sigmoid(x)


def _mod_specs(layer):
    one = lambda index: pl.BlockSpec((None, None, 1, 6 * D_MODEL), index)
    return [one(lambda b, t: (layer, b, 0, 0)), one(lambda b, t: (layer, MOD_ROWS // 2, 0, 0))]


def _row_groups(tile_rows):
    split = SEQ % tile_rows or tile_rows // 2
    return slice(0, split), slice(split, tile_rows)


def _group_mods(mod_ref, ctx_mod_ref, tile_rows, rows):
    first_row = pl.program_id(1) * tile_rows + rows.start
    return jnp.where(first_row >= SEQ, ctx_mod_ref[...], mod_ref[...])


MOD_COL_TILE = 1536


def _modulation_kernel(cond_ref, w_ref, b_ref, out_ref):
    a = _silu(cond_ref[...]).astype(BF16)
    out_ref[...] = _dot(a, w_ref[...].astype(BF16)) + b_ref[...]


def _modulation(cond, w_mod, b_mod):
    depth = w_mod.shape[0]
    n = w_mod.shape[2]
    return pl.pallas_call(
        _modulation_kernel,
        out_shape=jax.ShapeDtypeStruct((depth, MOD_ROWS, n), F32),
        grid=(depth, n // MOD_COL_TILE),
        in_specs=[
            pl.BlockSpec((MOD_ROWS, D_MODEL), lambda l, j: (0, 0)),
            pl.BlockSpec((None, D_MODEL, MOD_COL_TILE), lambda l, j: (l, 0, j)),
            pl.BlockSpec((None, 1, MOD_COL_TILE), lambda l, j: (l, 0, j)),
        ],
        out_specs=pl.BlockSpec((None, MOD_ROWS, MOD_COL_TILE), lambda l, j: (l, 0, j)),
        compiler_params=_params(("parallel", "parallel")),
        name="modulation",
    )(cond, w_mod, b_mod.reshape(depth, 1, n))


def _lane_index(shape):
    return lax.broadcasted_iota(jnp.int32, shape, len(shape) - 1)


def _head_mean_matrix():
    shift = HEAD_DIM.bit_length() - 1
    r = lax.broadcasted_iota(jnp.int32, (LANES, LANES), 0) >> shift
    c = lax.broadcasted_iota(jnp.int32, (LANES, LANES), 1) >> shift
    return jnp.where(r == c, 1.0 / HEAD_DIM, 0.0).astype(BF16)


def _head_rms_norm(u, gain, mean_bd):
    ms = _dot((u * u).astype(BF16), mean_bd)
    return u * lax.rsqrt(ms + EPS) * gain


def _rope(u, cos, sin_signed):
    first = (_lane_index(u.shape) & 31) < 16
    partner = jnp.where(first, pltpu.roll(u, LANES - 16, axis=1), pltpu.roll(u, 16, axis=1))
    return u * cos + partner * sin_signed


def _hgrn_lower_bound(lb_ref, layer, direction):
    depth = lb_ref.shape[0] // 2
    rows = [lb_ref[2 * j + direction:2 * j + direction + 1, :] for j in range(depth)]
    m = functools.reduce(jnp.maximum, rows)
    e = [jnp.exp(r - m) for r in rows]
    z = functools.reduce(lambda a, b: a + b, e)
    p = [x / z for x in e]
    csum = functools.reduce(lambda a, b: a + b, p[:layer + 1])
    return csum - p[0]


def _hgrn_forget(z, lb):
    t = jnp.exp(-jnp.abs(z))
    pos = z >= 0.0
    lb_floor = jnp.maximum(lb, LB_FLOOR)
    r = 1.0 / (1.0 + t)
    tr = t * r
    k = (1.0 - lb) * jnp.where(pos, tr, r)
    f = lb_floor + (1.0 - lb_floor) * jnp.where(pos, r, tr)
    return k, jnp.log(f) * LOG2E


GATE_CHUNKS = 6


def _inproj_kernel(x_ref, mod_ref, ctx_mod_ref, g1_ref, w_ref, qg_ref, kg_ref, cos_ref, sin_ref, lb_ref,
                   q_ref, k_ref, v_ref, hv_ref, hk_ref, hlf_ref, hq_ref, hg_ref, four_ref, gate_ref,
                   *, layer):
    tile_rows = x_ref.shape[1]
    groups = [dict(rows=rows) for rows in _row_groups(tile_rows)]
    mean_bd = _head_mean_matrix()
    q_gain = qg_ref[...] * (HEAD_DIM ** -0.5 * LOG2E)
    gate_width = N_BRANCH * D_MODEL // GATE_CHUNKS

    for st in groups:
        rows = st["rows"]
        mod = _group_mods(mod_ref, ctx_mod_ref, tile_rows, rows)
        x = x_ref[0, rows, :]
        ms = jnp.mean(x * x, axis=-1, keepdims=True)
        row_gain = g1_ref[...] * (1.0 + mod[:, D_MODEL:2 * D_MODEL])
        st["h"] = (x * lax.rsqrt(ms + EPS) * row_gain + mod[:, 0:D_MODEL]).astype(BF16)
        st["cos"] = cos_ref[rows, :]
        st["sin"] = sin_ref[rows, :]
        st["low_head"] = _lane_index((rows.stop - rows.start, LANES)) < HEAD_DIM

    def proj(st, seg, width=None):
        a = _OFF[seg]
        b = _OFF[seg + 1] if width is None else a + width
        return _dot(st["h"], w_ref[:, a:b])

    def gate_chunk(st, i):
        a = _OFF[9] + i * gate_width
        gate_ref[0, st["rows"], i * gate_width:(i + 1) * gate_width] = _dot(
            st["h"], w_ref[:, a:a + gate_width]).astype(BF16)

    for st in groups:
        st["kv"] = proj(st, 0, 2 * KV_WIDTH)
        gate_chunk(st, 0)
    for st in groups:
        rows, low_head, kv = st["rows"], st["low_head"], st.pop("kv")
        kr = _rope(_head_rms_norm(kv[:, :KV_WIDTH], kg_ref[...], mean_bd), st["cos"], st["sin"])
        kr_sw = pltpu.roll(kr, HEAD_DIM, axis=1)
        k_ref[0, 0, rows, :] = jnp.where(low_head, kr, kr_sw).astype(BF16)
        k_ref[0, 1, rows, :] = jnp.where(low_head, kr_sw, kr).astype(BF16)
        vr = kv[:, KV_WIDTH:]
        vr_sw = pltpu.roll(vr, HEAD_DIM, axis=1)
        v_ref[0, 0, rows, :] = jnp.where(low_head, vr, vr_sw).astype(BF16)
        v_ref[0, 1, rows, :] = jnp.where(low_head, vr_sw, vr).astype(BF16)

    for st in groups:
        st["q"] = proj(st, 5)
        gate_chunk(st, 1)
    for st in groups:
        qr = st.pop("q")
        for c in range(BRANCH_WIDTH // LANES):
            u = qr[:, c * LANES:(c + 1) * LANES]
            q_ref[0, st["rows"], c * LANES:(c + 1) * LANES] = _rope(
                _head_rms_norm(u, q_gain, mean_bd), st["cos"], st["sin"]).astype(BF16)

    for direction, seg in ((0, 3), (1, 4)):
        lb = _hgrn_lower_bound(lb_ref, layer, direction)
        cols = slice(direction * BRANCH_WIDTH, (direction + 1) * BRANCH_WIDTH)
        for st in groups:
            st["z"] = proj(st, seg)
            gate_chunk(st, 2 + direction)
        for st in groups:
            k, log2f = _hgrn_forget(st.pop("z"), lb)
            hk_ref[0, st["rows"], cols] = k.astype(BF16)
            hlf_ref[0, st["rows"], cols] = log2f
    for st in groups:
        st["qh"] = proj(st, 6)
        gate_chunk(st, 4)
    for st in groups:
        hq_ref[0, st["rows"], :] = _silu(st.pop("qh")).astype(BF16)
    for st in groups:
        rows = st["rows"]
        hv_ref[0, rows, :] = proj(st, 2).astype(BF16)
        hg_ref[0, rows, :] = proj(st, 7).astype(BF16)
        four_ref[0, rows, :] = proj(st, 8).astype(BF16)
        gate_chunk(st, 5)


def _inproj(xs, mods, g1, w_in, qg, kg, cos, sin, lb_logits, *, layer):
    bsz = xs.shape[0]
    row = lambda w: pl.BlockSpec((1, ROW_TILE, w), lambda b, t: (b, t, 0))
    dup = pl.BlockSpec((1, ATTN_KV_HEADS, ROW_TILE, LANES), lambda b, t: (b, 0, t, 0))
    tok = lambda w, dt: jax.ShapeDtypeStruct((bsz, TOKENS, w), dt)
    dup_shape = jax.ShapeDtypeStruct((bsz, ATTN_KV_HEADS, TOKENS, LANES), BF16)
    return pl.pallas_call(
        functools.partial(_inproj_kernel, layer=layer),
        out_shape=(tok(BRANCH_WIDTH, BF16), dup_shape, dup_shape,
                   tok(BRANCH_WIDTH, BF16), tok(2 * BRANCH_WIDTH, BF16), tok(2 * BRANCH_WIDTH, F32),
                   tok(BRANCH_WIDTH, BF16), tok(BRANCH_WIDTH, BF16), tok(BRANCH_WIDTH, BF16),
                   tok(N_BRANCH * D_MODEL, BF16)),
        grid=(bsz, ROW_TILES),
        in_specs=[
            row(D_MODEL),
            *_mod_specs(layer),
            _layer_spec(g1.shape, layer),
            _layer_spec(w_in.shape, layer),
            _layer_spec(qg.shape, layer),
            _layer_spec(kg.shape, layer),
            pl.BlockSpec((ROW_TILE, LANES), lambda b, t: (t, 0)),
            pl.BlockSpec((ROW_TILE, LANES), lambda b, t: (t, 0)),
            _const_spec(lb_logits.shape),
        ],
        out_specs=(row(BRANCH_WIDTH), dup, dup,
                   row(BRANCH_WIDTH), row(2 * BRANCH_WIDTH), row(2 * BRANCH_WIDTH),
                   row(BRANCH_WIDTH), row(BRANCH_WIDTH), row(BRANCH_WIDTH),
                   row(N_BRANCH * D_MODEL)),
        compiler_params=_params(("parallel", "parallel")),
        name="inproj",
    )(xs, mods, mods, g1, w_in, qg, kg, cos, sin, lb_logits)


ATTN_BLOCKS = TOKENS // ATTN_BLOCK
LATENT_ATTN_BLOCKS = SEQ // ATTN_BLOCK
GROUP = ATTN_HEADS // ATTN_KV_HEADS
PAIR_COLS = 2 * ATTN_BLOCK
ATTN_STEP_BLOCKS = 6
assert ATTN_BLOCKS % ATTN_STEP_BLOCKS == 0


def _attend_heads(sink_ref, q_ref, o_ref, query_blocks):
    low_head = _lane_index((ATTN_BLOCK, LANES)) < HEAD_DIM
    zero = jnp.zeros((), BF16)
    odd_cols = lax.broadcasted_iota(jnp.int32, (1, PAIR_COLS), 1) >= ATTN_BLOCK
    tasks = []
    for n, (kv_blocks, masks) in enumerate(query_blocks):
        kv = [kv_blocks(hk) for hk in range(ATTN_KV_HEADS)]
        rows = slice(n * ATTN_BLOCK, (n + 1) * ATTN_BLOCK)
        tasks += [dict(pair=c, rows=rows, kv=kv[(2 * c) // GROUP], masks=masks) for c in range(ATTN_HEADS // 2)]

    def scores_stage(st):
        c = st["pair"]
        tile = q_ref[0, st["rows"], c * LANES:(c + 1) * LANES]
        qs = jnp.concatenate([jnp.where(low_head, tile, zero), jnp.where(low_head, zero, tile)], axis=0)
        scores = [_dot_nt(kb, qs) for kb in st["kv"][0]]
        st["scores"] = [s if mk is None else jnp.where(mk, s, NEG_BIG) for s, mk in zip(scores, st["masks"])]
        st["sink"] = jnp.where(odd_cols, sink_ref[2 * c + 1], sink_ref[2 * c]) * LOG2E

    def max_stage(st):
        m = st["sink"]
        for s in st["scores"]:
            m = jnp.maximum(m, jnp.max(s, axis=0, keepdims=True))
        st["m"] = m

    def values_stage(st):
        denom = jnp.exp2(st["sink"] - st["m"])
        o = None
        for s, vb in zip(st["scores"], st["kv"][1]):
            p = jnp.exp2(s - st["m"])
            denom = denom + jnp.sum(p, axis=0, keepdims=True)
            pv = _dot_tn(vb, p.astype(BF16))
            o = pv if o is None else o + pv
        o = o / denom
        c = st["pair"]
        o_ref[0, st["rows"], c * LANES:(c + 1) * LANES] = jnp.where(
            low_head, o[:, :ATTN_BLOCK].T, o[:, ATTN_BLOCK:].T).astype(BF16)

    stages = (scores_stage, max_stage, values_stage)
    for tick in range(len(tasks) + len(stages) - 1):
        for depth, stage in enumerate(stages):
            if 0 <= tick - depth < len(tasks):
                stage(tasks[tick - depth])


def _attention_kernel(sink_ref, q_ref, k_ref, v_ref, o_ref):
    j2 = lax.broadcasted_iota(jnp.int32, (2 * ATTN_BLOCK, PAIR_COLS), 0)
    t2 = lax.broadcasted_iota(jnp.int32, (2 * ATTN_BLOCK, PAIR_COLS), 1) & (ATTN_BLOCK - 1)
    j1 = lax.broadcasted_iota(jnp.int32, (ATTN_BLOCK, PAIR_COLS), 0)
    t1 = lax.broadcasted_iota(jnp.int32, (ATTN_BLOCK, PAIR_COLS), 1) & (ATTN_BLOCK - 1)

    def query_block(n):
        i = pl.program_id(1) * ATTN_STEP_BLOCKS + n
        prev_start = pl.multiple_of(jnp.maximum(i - 1, 0) * ATTN_BLOCK, ATTN_BLOCK)
        cur_start = pl.multiple_of(i * ATTN_BLOCK, ATTN_BLOCK)
        next_start = pl.multiple_of(jnp.minimum(i + 1, LATENT_ATTN_BLOCKS - 1) * ATTN_BLOCK, ATTN_BLOCK)

        def blocks(ref, hk):
            near = jnp.concatenate([ref[0, hk, pl.ds(prev_start, ATTN_BLOCK), :],
                                    ref[0, hk, pl.ds(cur_start, ATTN_BLOCK), :]], axis=0)
            return near, ref[0, hk, pl.ds(next_start, ATTN_BLOCK), :], ref[0, hk, SEQ:TOKENS, :]

        in_range = jnp.logical_or(j2 >= ATTN_BLOCK, jnp.logical_and(j2 >= t2, i > 0))
        near_valid = jnp.logical_and(in_range, i < LATENT_ATTN_BLOCKS)
        next_valid = jnp.logical_and(j1 <= t1, i < LATENT_ATTN_BLOCKS - 1)
        return (lambda hk: (blocks(k_ref, hk), blocks(v_ref, hk))), (near_valid, next_valid, None)

    _attend_heads(sink_ref, q_ref, o_ref, [query_block(n) for n in range(ATTN_STEP_BLOCKS)])


def _attention(q, k_dup, v_dup, sink):
    bsz = q.shape[0]
    stream = pl.BlockSpec((1, ATTN_KV_HEADS, TOKENS, LANES), lambda b, i: (b, 0, 0, 0))
    step_rows = pl.BlockSpec((1, ATTN_STEP_BLOCKS * ATTN_BLOCK, BRANCH_WIDTH), lambda b, i: (b, i, 0))
    return pl.pallas_call(
        _attention_kernel,
        out_shape=jax.ShapeDtypeStruct((bsz, TOKENS, BRANCH_WIDTH), BF16),
        grid=(bsz, ATTN_BLOCKS // ATTN_STEP_BLOCKS),
        in_specs=[pl.BlockSpec(memory_space=pltpu.SMEM), step_rows, stream, stream],
        out_specs=step_rows,
        compiler_params=_params(("parallel", "arbitrary")),
        name="attention",
    )(sink, q, k_dup, v_dup)


def _cumsum_rows(tri2, x):
    hi = x.astype(BF16)
    lo = (x - hi.astype(F32)).astype(BF16)
    return _dot(tri2, jnp.concatenate([hi, lo], axis=0))


def _hgrn_exact_chunk(reverse, rows, slot, q_ref, k_ref, lf_ref, v_ref, o_ref, old_ref, new_ref, row_ref, acc_ref):
    n = HGRN_CHUNK
    rr = lax.broadcasted_iota(jnp.int32, (n, n), 0)
    cc = lax.broadcasted_iota(jnp.int32, (n, n), 1)
    tri = jnp.where((cc >= rr) if reverse else (cc <= rr), 1.0, 0.0).astype(BF16)
    tri2 = jnp.concatenate([tri, tri], axis=1)
    last_row = 0 if reverse else n - 1
    s_idx = lax.broadcasted_iota(jnp.int32, (n, 1), 0)

    def batch_row(bi, carry):
        g_all = _cumsum_rows(tri2, lf_ref[bi, rows, :])
        for hd in range(HGRN_HEADS):
            ln = slice(hd * HGRN_DK, (hd + 1) * HGRN_DK)
            g = g_all[:, ln]
            q = q_ref[bi, rows, ln].astype(F32)
            k = k_ref[bi, rows, ln].astype(F32)
            v = v_ref[bi, rows, ln]
            vf = v.astype(F32)
            st = old_ref[bi, slot, hd]
            o_inter = _dot_nt((q * jnp.exp2(g)).astype(BF16), st.astype(BF16))
            row_ref[0] = g
            row_ref[1] = q

            def row(t, c):
                g_t = row_ref[0, pl.ds(t, 1), :]
                q_t = row_ref[1, pl.ds(t, 1), :]
                w = jnp.sum(q_t * k * jnp.exp2(jnp.minimum(g_t - g, 0.0)), axis=1, keepdims=True)
                before = (s_idx >= t) if reverse else (s_idx <= t)
                acc_ref[pl.ds(t, 1), :] = jnp.sum(jnp.where(before, w, 0.0) * vf, axis=0, keepdims=True)
                return c

            lax.fori_loop(0, n, row, 0)
            o_ref[bi, rows, ln] = (o_inter + acc_ref[...]).astype(o_ref.dtype)
            g_tot = g[last_row:last_row + 1, :]
            kd = (k * jnp.exp2(g_tot - g)).astype(BF16)
            new_ref[bi, slot, hd] = st * jnp.exp2(g_tot) + _dot_tn(v, kd)
        return carry

    lax.fori_loop(0, HGRN_BATCH, batch_row, 0)


def _hgrn_chunk(directions, old_ref, new_ref, row_ref, acc_ref):
    n = HGRN_CHUNK
    rr = lax.broadcasted_iota(jnp.int32, (n, n), 0)
    cc = lax.broadcasted_iota(jnp.int32, (n, n), 1)

    chains = []
    for slot, (reverse, rows, q_ref, k_ref, lf_ref, v_ref, o_ref) in enumerate(directions):
        causal = (cc >= rr) if reverse else (cc <= rr)
        last_row, mid_row = (0, n // 2) if reverse else (n - 1, n // 2 - 1)
        tri = jnp.where(causal, 1.0, 0.0).astype(BF16)
        tri2 = jnp.concatenate([tri, tri], axis=1)
        for bi in range(HGRN_BATCH):
            g_all = _cumsum_rows(tri2, lf_ref[bi, rows, :])
            for hd in range(HGRN_HEADS):
                ln = slice(hd * HGRN_DK, (hd + 1) * HGRN_DK)
                g = g_all[:, ln]
                chains.append(dict(
                    causal=causal, g=g, g_tot=g[last_row:last_row + 1, :], g_mid=g[mid_row:mid_row + 1, :],
                    q=q_ref[bi, rows, ln], k=k_ref[bi, rows, ln], v=v_ref[bi, rows, ln],
                    o_ref=o_ref, state=(bi, slot, hd), out=(bi, rows, ln)))

    spread = [None] * len(directions)
    for ch in chains:
        d = ch["g"] - ch["g_mid"]
        slot = ch["state"][1]
        spread[slot] = jnp.abs(d) if spread[slot] is None else jnp.maximum(spread[slot], jnp.abs(d))
        ch["qa"] = ch["q"] * jnp.exp2(jnp.minimum(d, EXP2_CLAMP)).astype(BF16)
        ch["kb"] = ch["k"] * jnp.exp2(jnp.minimum(-d, EXP2_CLAMP)).astype(BF16)
        ch["st_mid"] = old_ref[ch["state"]] * jnp.exp2(ch["g_mid"])
    for ch in chains:
        ch["a"] = _dot_nt(ch["qa"], ch["kb"])
        ch["o_inter"] = _dot_nt(ch["qa"], ch["st_mid"].astype(BF16))
        ch["u"] = _dot_tn(ch["v"], ch["kb"])
    for ch in chains:
        a = jnp.where(ch["causal"], ch["a"], 0.0).astype(BF16)
        ch["o_ref"][ch["out"]] = (ch["o_inter"] + _dot(a, ch["v"])).astype(ch["o_ref"].dtype)
        new_ref[ch["state"]] = (ch["st_mid"] + ch["u"]) * jnp.exp2(ch["g_tot"] - ch["g_mid"])

    for slot, ((reverse, rows, q_ref, k_ref, lf_ref, v_ref, o_ref), worst) in enumerate(zip(directions, spread)):
        @pl.when(jnp.max(worst) > EXP2_CLAMP)
        def _():
            _hgrn_exact_chunk(reverse, rows, slot, q_ref, k_ref, lf_ref, v_ref, o_ref, old_ref, new_ref,
                              row_ref, acc_ref)


def _hgrn_kernel(qf_ref, kf_ref, lf_ref, vf_ref, qb_ref, kb_ref, lb_ref, vb_ref, of_ref, ob_ref,
                 state_ref, row_ref, acc_ref):
    @pl.when(pl.program_id(1) == 0)
    def _():
        state_ref[0] = jnp.zeros(state_ref.shape[1:], F32)

    first, second = slice(0, HGRN_CHUNK), slice(HGRN_CHUNK, 2 * HGRN_CHUNK)
    for sub in range(2):
        directions = ((False, (first, second)[sub], qf_ref, kf_ref, lf_ref, vf_ref, of_ref),
                      (True, (second, first)[sub], qb_ref, kb_ref, lb_ref, vb_ref, ob_ref))
        _hgrn_chunk(directions, state_ref.at[sub], state_ref.at[1 - sub], row_ref, acc_ref)


HGRN_STEP_ROWS = 2 * HGRN_CHUNK
HGRN_STEPS = TOKENS // HGRN_STEP_ROWS
HGRN_CTX_STEPS = CTX_LEN // HGRN_STEP_ROWS


def _hgrn(hq, hk, hlf, hv):
    bsz = hq.shape[0]
    assert bsz % HGRN_BATCH == 0
    lat = HGRN_STEPS - HGRN_CTX_STEPS

    def fwd(c):
        return jnp.where(c < HGRN_CTX_STEPS, lat + c, c - HGRN_CTX_STEPS)

    def bwd(c):
        return HGRN_STEPS - 1 - c

    def spec(order, lane_block):
        return pl.BlockSpec((HGRN_BATCH, HGRN_STEP_ROWS, BRANCH_WIDTH), lambda b, c: (b, order(c), lane_block))

    out = jax.ShapeDtypeStruct((bsz, TOKENS, BRANCH_WIDTH), BF16)
    return pl.pallas_call(
        _hgrn_kernel,
        out_shape=(out, out),
        grid=(bsz // HGRN_BATCH, HGRN_STEPS),
        in_specs=[spec(fwd, 0), spec(fwd, 0), spec(fwd, 0), spec(fwd, 0),
                  spec(bwd, 0), spec(bwd, 1), spec(bwd, 1), spec(bwd, 0)],
        out_specs=(spec(fwd, 0), spec(bwd, 0)),
        scratch_shapes=[pltpu.VMEM((2, HGRN_BATCH, 2, HGRN_HEADS, HGRN_DK, HGRN_DK), F32),
                        pltpu.VMEM((2, HGRN_CHUNK, HGRN_DK), F32), pltpu.VMEM((HGRN_CHUNK, HGRN_DK), F32)],
        compiler_params=_params(("parallel", "arbitrary")),
        name="hgrn",
    )(hq, hk, hlf, hv, hq, hk, hlf, hv)


FOURIER_ROW_TILE = 768
FOURIER_TILES = TOKENS // FOURIER_ROW_TILE
FOURIER_LAST_LATENT = SEQ - (FOURIER_TILES - 1) * FOURIER_ROW_TILE
assert FOURIER_TILES * FOURIER_ROW_TILE == TOKENS and FOURIER_LAST_LATENT + CTX_LEN == FOURIER_ROW_TILE


HALF_SEQ = SEQ // 2


def _fourier_kernel(x_ref, xc_ref, wc_ref, cs_ref, csc_ref, o_ref, uv_ref, nyq_ref):
    t = pl.program_id(1)
    last = FOURIER_TILES - 1
    latent_scale = 1.0 / math.sqrt(SEQ * FOURIER_GROUP_DIM)
    blk = LANES

    @pl.when(t == 0)
    def _():
        rr = lax.broadcasted_iota(jnp.int32, (blk, blk), 0)
        cc = lax.broadcasted_iota(jnp.int32, (blk, blk), 1)
        flip = jnp.where(rr + cc == blk, 1.0, 0.0).astype(BF16)
        first_row = lax.broadcasted_iota(jnp.int32, (blk, 1), 0) == 0
        n_blk = HALF_SEQ // blk
        cos_c = wc_ref[:, :BRANCH_WIDTH]
        sin_c = wc_ref[:, BRANCH_WIDTH:]
        for a in range(n_blk):
            src = HALF_SEQ + (n_blk - 1 - a) * blk
            z = _dot(flip, x_ref[0, src:src + blk, :])
            if a > 0:
                z = jnp.where(first_row, x_ref[0, src + blk:src + blk + 16, :].astype(F32)[0:1, :], z)
            lo = x_ref[0, a * blk:(a + 1) * blk, :].astype(F32)
            uv_ref[a * blk:(a + 1) * blk, :] = _dot((lo + z).astype(BF16), cos_c).astype(BF16)
            uv_ref[HALF_SEQ + a * blk:HALF_SEQ + (a + 1) * blk, :] = _dot((lo - z).astype(BF16), sin_c).astype(BF16)
        nyq_ref[...] = _dot(x_ref[0, HALF_SEQ:HALF_SEQ + nyq_ref.shape[0], :], cos_c)

    def latent_rows(r0, n):
        sign = 1.0 - 2.0 * ((r0 + lax.broadcasted_iota(jnp.int32, (n, 1), 0)) & 1).astype(F32)
        return (_dot(cs_ref[pl.ds(r0, n), :], uv_ref[...]) + sign * nyq_ref[0:1, :]) * latent_scale

    @pl.when(t < last)
    def _():
        r0 = pl.multiple_of(t * FOURIER_ROW_TILE, FOURIER_ROW_TILE)
        o_ref[0] = latent_rows(r0, FOURIER_ROW_TILE).astype(BF16)

    @pl.when(t == last)
    def _():
        o_ref[0, 0:FOURIER_LAST_LATENT, :] = latent_rows(last * FOURIER_ROW_TILE, FOURIER_LAST_LATENT).astype(BF16)
        uv = _dot(xc_ref[0], wc_ref[...])
        uvc = jnp.concatenate([uv[:, :BRANCH_WIDTH], uv[:, BRANCH_WIDTH:]], axis=0).astype(BF16)
        yc = _dot(csc_ref[...], uvc)
        o_ref[0, FOURIER_LAST_LATENT:FOURIER_ROW_TILE, :] = (
            yc * (1.0 / math.sqrt(CTX_LEN * FOURIER_GROUP_DIM))).astype(BF16)


def _fourier(four, wc, cs_lat, cs_ctx):
    bsz = four.shape[0]
    return pl.pallas_call(
        _fourier_kernel,
        out_shape=jax.ShapeDtypeStruct((bsz, TOKENS, BRANCH_WIDTH), BF16),
        grid=(bsz, FOURIER_TILES),
        in_specs=[pl.BlockSpec((1, SEQ, BRANCH_WIDTH), lambda b, t: (b, 0, 0)),
                  pl.BlockSpec((1, CTX_LEN, BRANCH_WIDTH), lambda b, t: (b, SEQ // CTX_LEN, 0)),
                  _const_spec(wc.shape), _const_spec(cs_lat.shape), _const_spec(cs_ctx.shape)],
        out_specs=pl.BlockSpec((1, FOURIER_ROW_TILE, BRANCH_WIDTH), lambda b, t: (b, t, 0)),
        scratch_shapes=[pltpu.VMEM((2 * HALF_SEQ, BRANCH_WIDTH), BF16), pltpu.VMEM((16, BRANCH_WIDTH), F32)],
        compiler_params=_params(("parallel", "arbitrary")),
        name="fourier",
    )(four, four, wc, cs_lat, cs_ctx)


def _dft_tables(n, split, n_cols):
    col = jnp.arange(n_cols, dtype=jnp.int32)

    def trig(row_factor):
        ang = ((row_factor[:, None] * col[None, :]) % n).astype(F32) * (2.0 * math.pi / n)
        return jnp.cos(ang), jnp.sin(ang)

    cos_a, sin_a = trig(split * jnp.arange(n // split, dtype=jnp.int32))
    cos_b, sin_b = trig(jnp.arange(split, dtype=jnp.int32))
    cos = cos_a[:, None, :] * cos_b[None, :, :] - sin_a[:, None, :] * sin_b[None, :, :]
    sin = sin_a[:, None, :] * cos_b[None, :, :] + cos_a[:, None, :] * sin_b[None, :, :]
    return jnp.concatenate([cos.reshape(n, n_cols), -sin.reshape(n, n_cols)], axis=1).astype(BF16)


def _channel_dft_table():
    c = jnp.arange(BRANCH_WIDTH, dtype=jnp.int32)
    same = (c[:, None] // FOURIER_GROUP_DIM) == (c[None, :] // FOURIER_GROUP_DIM)
    ang = (((c[:, None] % FOURIER_GROUP_DIM) * (c[None, :] % FOURIER_GROUP_DIM)) % FOURIER_GROUP_DIM
           ).astype(F32) * (2.0 * math.pi / FOURIER_GROUP_DIM)
    cos = jnp.where(same, jnp.cos(ang), 0.0)
    sin = jnp.where(same, jnp.sin(ang), 0.0)
    return jnp.concatenate([cos, sin], axis=1).astype(BF16)


FF_CHUNK = 1024


def _rms(x):
    return x * lax.rsqrt(jnp.mean(x * x, axis=-1, keepdims=True) + EPS)


def _merge_mlp_kernel(x_ref, mod_ref, ctx_mod_ref, of_ref, oa_ref, hf_ref, hb_ref, hg_ref, gate_ref,
                      hng_ref, g2_ref, wb_ref, wo_ref, w1_ref, w2_ref, out_ref):
    d = D_MODEL
    tile_rows = x_ref.shape[1]
    groups = [dict(rows=rows) for rows in _row_groups(tile_rows)]

    def mix_stage(st):
        rows = st["rows"]
        mods = _group_mods(mod_ref, ctx_mod_ref, tile_rows, rows)
        mod = st["mod"] = lambda i: mods[:, i * d:(i + 1) * d]
        o_h = hf_ref[0, rows, :].astype(F32) + hb_ref[0, rows, :].astype(F32)
        o_n = jnp.concatenate(
            [_rms(o_h[:, hd * HGRN_DK:(hd + 1) * HGRN_DK]) * hng_ref[...] for hd in range(HGRN_HEADS)], axis=-1)
        o_hr = (o_n * _silu(hg_ref[0, rows, :].astype(F32))).astype(BF16)
        branches = (of_ref[0, rows, :], oa_ref[0, rows, :], o_hr)
        mix = None
        for n, br in enumerate(branches):
            term = _sigmoid(gate_ref[0, rows, n * d:(n + 1) * d].astype(F32)) * _dot(br, wb_ref[n])
            mix = term if mix is None else mix + term
        y = _dot(mix.astype(BF16), wo_ref[...])
        st["x1"] = x1 = x_ref[0, rows, :] + mod(2) * y
        st["h2"] = (_rms(x1) * (g2_ref[...] * (1.0 + mod(4))) + mod(3)).astype(BF16)

    def mlp_stage(st):
        ff = None
        for c in range(D_FF // FF_CHUNK):
            cols = slice(c * FF_CHUNK, (c + 1) * FF_CHUNK)
            u = jnp.square(jnp.maximum(_dot(st["h2"], w1_ref[:, cols]), 0.0)).astype(BF16)
            part = _dot(u, w2_ref[cols, :])
            ff = part if ff is None else ff + part
        out_ref[0, st["rows"], :] = st["x1"] + st["mod"](5) * ff

    for stage in (mix_stage, mlp_stage):
        for st in groups:
            stage(st)


def _merge_mlp(xs, mods, o_four, o_attn, o_hf, o_hb, hg, gate, hng, g2, wb, wo, w1, w2, *, layer, latent_only):
    bsz = xs.shape[0]
    rows = LATENT_ROW_TILE if latent_only else ROW_TILE
    n_rows = SEQ if latent_only else TOKENS
    row = lambda w: pl.BlockSpec((1, rows, w), lambda b, t: (b, t, 0))
    return pl.pallas_call(
        _merge_mlp_kernel,
        out_shape=jax.ShapeDtypeStruct((bsz, n_rows, D_MODEL), F32),
        grid=(bsz, n_rows // rows),
        in_specs=[row(D_MODEL),
                  *_mod_specs(layer),
                  row(BRANCH_WIDTH), row(BRANCH_WIDTH), row(BRANCH_WIDTH), row(BRANCH_WIDTH),
                  row(BRANCH_WIDTH), row(N_BRANCH * D_MODEL),
                  *[_layer_spec(a.shape, layer) for a in (hng, g2, wb, wo, w1, w2)]],
        out_specs=row(D_MODEL),
        compiler_params=_params(("parallel", "parallel")),
        name="merge_mlp",
    )(xs, mods, mods, o_four, o_attn, o_hf, o_hb, hg, gate, hng, g2, wb, wo, w1, w2)


def _rope_tables():
    pos = jnp.arange(SEQ)
    row = (pos // GRID_W).astype(F32)
    col = (pos % GRID_W).astype(F32)
    axis_dim = HEAD_DIM // 2
    half = axis_dim // 2
    inv_freq = ROPE_THETA ** (-jnp.arange(0, axis_dim, 2, dtype=F32) / axis_dim)
    ang_r = row[:, None] * inv_freq
    ang_c = col[:, None] * inv_freq
    ang = jnp.concatenate([ang_r, ang_r, ang_c, ang_c], axis=1)
    sign = jnp.tile(jnp.concatenate([-jnp.ones(half, F32), jnp.ones(half, F32)]), 2)
    cos = jnp.concatenate([jnp.cos(ang), jnp.ones((CTX_LEN, HEAD_DIM), F32)], axis=0)
    sin = jnp.concatenate([jnp.sin(ang) * sign, jnp.zeros((CTX_LEN, HEAD_DIM), F32)], axis=0)
    return jnp.tile(cos, (1, 2)), jnp.tile(sin, (1, 2))


def kernel(x, c, ctx, c_ctx, w_mod, b_mod, norm1_g, norm2_g, w_in, q_norm_g, k_norm_g, attn_sink,
           hgrn_lb_logits, hgrn_norm_g, w_branch, w_out, w_ff1, w_ff2):
    bsz, n_tok, d = x.shape
    depth = w_mod.shape[0]
    assert (n_tok, d, ctx.shape[1]) == (SEQ, D_MODEL, CTX_LEN) and bsz <= MOD_ROWS // 2

    xs = jnp.concatenate([x, ctx], axis=1)
    cond = jnp.zeros((MOD_ROWS, d), F32).at[:bsz].set(c).at[MOD_ROWS // 2].set(c_ctx)
    mods = _modulation(cond, w_mod, b_mod).reshape(depth, MOD_ROWS, 1, 6 * d)

    cos, sin = _rope_tables()
    wc = _channel_dft_table()
    cs_lat = _dft_tables(SEQ, 32, HALF_SEQ)
    cs_ctx = _dft_tables(CTX_LEN, 16, CTX_LEN)
    lb_logits = hgrn_lb_logits.reshape(depth * 2, BRANCH_WIDTH)

    g1 = norm1_g.reshape(depth, 1, d)
    g2 = norm2_g.reshape(depth, 1, d)
    qg = jnp.tile(q_norm_g, (1, 2)).reshape(depth, 1, LANES)
    kg = jnp.tile(k_norm_g, (1, 2)).reshape(depth, 1, LANES)
    hng = hgrn_norm_g.reshape(depth, 1, HGRN_DK)
    w_in, w_branch, w_out, w_ff1, w_ff2 = (w.astype(BF16) for w in (w_in, w_branch, w_out, w_ff1, w_ff2))

    for l in range(depth):
        q, k_dup, v_dup, hv, hk, hlf, hq, hg, four, gate = _inproj(
            xs, mods, g1, w_in, qg, kg, cos, sin, lb_logits, layer=l)
        o_attn = _attention(q, k_dup, v_dup, attn_sink[l])
        o_hf, o_hb = _hgrn(hq, hk, hlf, hv)
        o_four = _fourier(four, wc, cs_lat, cs_ctx)
        xs = _merge_mlp(xs, mods, o_four, o_attn, o_hf, o_hb, hg, gate, hng, g2,
                        w_branch, w_out, w_ff1, w_ff2, layer=l, latent_only=(l == depth - 1))
    return xs
```

```python
import functools
import math

import jax
import jax.numpy as jnp
from jax import lax
from jax.experimental import pallas as pl
from jax.experimental.pallas import tpu as pltpu

F32 = jnp.float32
BF16 = jnp.bfloat16

D_MODEL = 1024
SEQ = 2048
CTX_LEN = 256
TOKENS = SEQ + CTX_LEN
GRID_W = 64
EPS = 1e-6
NEG_BIG = -1e30
LB_FLOOR = 1e-30
ROPE_THETA = 10000.0

BRANCH_WIDTH = D_MODEL // 2
HEAD_DIM = 64
ATTN_HEADS = BRANCH_WIDTH // HEAD_DIM
ATTN_KV_HEADS = ATTN_HEADS // 4
KV_WIDTH = ATTN_KV_HEADS * HEAD_DIM
ATTN_BLOCK = 128
FOURIER_GROUPS = 4
FOURIER_GROUP_DIM = BRANCH_WIDTH // FOURIER_GROUPS
HGRN_HEADS = 4
HGRN_DK = BRANCH_WIDTH // HGRN_HEADS
D_FF = 4 * D_MODEL
N_BRANCH = 3

_SEG = (KV_WIDTH, KV_WIDTH, BRANCH_WIDTH, BRANCH_WIDTH, BRANCH_WIDTH,
        BRANCH_WIDTH, BRANCH_WIDTH, BRANCH_WIDTH, BRANCH_WIDTH, N_BRANCH * D_MODEL)
_OFF = [sum(_SEG[:i]) for i in range(len(_SEG) + 1)]
D_IN = _OFF[-1]

LANES = 128
VMEM_LIMIT_BYTES = 56 * 1024 * 1024

ROW_TILE = 576
ROW_TILES = TOKENS // ROW_TILE
LATENT_ROW_TILE = 512
assert ROW_TILES * ROW_TILE == TOKENS and SEQ % LATENT_ROW_TILE == 0
HGRN_CHUNK = 64
HGRN_BATCH = 8
EXP2_CLAMP = 100.0
LOG2E = 1.0 / math.log(2.0)
MOD_ROWS = 16


def _const_spec(shape):
    nd = len(shape)
    return pl.BlockSpec(shape, lambda *_: (0,) * nd, pipeline_mode=pl.Buffered(1))


def _layer_spec(stacked_shape, layer):
    rest = tuple(stacked_shape[1:])
    return pl.BlockSpec((None,) + rest, lambda *_: (layer,) + (0,) * len(rest), pipeline_mode=pl.Buffered(1))


def _params(semantics, flags=None):
    return pltpu.CompilerParams(dimension_semantics=semantics, vmem_limit_bytes=VMEM_LIMIT_BYTES,
                                flags=flags)


def _dot(a, b):
    return jnp.dot(a, b, preferred_element_type=F32)


def _dot_nt(a, b):
    return lax.dot_general(a, b, (((1,), (1,)), ((), ())), preferred_element_type=F32)


def _dot_tn(a, b):
    return lax.dot_general(a, b, (((0,), (0,)), ((), ())), preferred_element_type=F32)


def _sigmoid(x):
    return 0.5 * jnp.tanh(0.5 * x) + 0.5


def _silu(x):
    return x * _sigmoid(x)


def _mod_specs(layer):
    one = lambda index: pl.BlockSpec((None, None, 1, 6 * D_MODEL), index)
    return [one(lambda b, t: (layer, b, 0, 0)), one(lambda b, t: (layer, MOD_ROWS // 2, 0, 0))]


def _row_groups(tile_rows):
    split = SEQ % tile_rows or tile_rows // 2
    return slice(0, split), slice(split, tile_rows)


def _group_mods(mod_ref, ctx_mod_ref, tile_rows, rows):
    first_row = pl.program_id(1) * tile_rows + rows.start
    return jnp.where(first_row >= SEQ, ctx_mod_ref[...], mod_ref[...])


MOD_COL_TILE = 1536


def _modulation_kernel(cond_ref, w_ref, b_ref, out_ref):
    a = _silu(cond_ref[...]).astype(BF16)
    out_ref[...] = _dot(a, w_ref[...].astype(BF16)) + b_ref[...]


def _modulation(cond, w_mod, b_mod):
    depth = w_mod.shape[0]
    n = w_mod.shape[2]
    return pl.pallas_call(
        _modulation_kernel,
        out_shape=jax.ShapeDtypeStruct((depth, MOD_ROWS, n), F32),
        grid=(depth, n // MOD_COL_TILE),
        in_specs=[
            pl.BlockSpec((MOD_ROWS, D_MODEL), lambda l, j: (0, 0)),
            pl.BlockSpec((None, D_MODEL, MOD_COL_TILE), lambda l, j: (l, 0, j)),
            pl.BlockSpec((None, 1, MOD_COL_TILE), lambda l, j: (l, 0, j)),
        ],
        out_specs=pl.BlockSpec((None, MOD_ROWS, MOD_COL_TILE), lambda l, j: (l, 0, j)),
        compiler_params=_params(("parallel", "parallel")),
        name="modulation",
    )(cond, w_mod, b_mod.reshape(depth, 1, n))


def _lane_index(shape):
    return lax.broadcasted_iota(jnp.int32, shape, len(shape) - 1)


def _head_mean_matrix():
    shift = HEAD_DIM.bit_length() - 1
    r = lax.broadcasted_iota(jnp.int32, (LANES, LANES), 0) >> shift
    c = lax.broadcasted_iota(jnp.int32, (LANES, LANES), 1) >> shift
    return jnp.where(r == c, 1.0 / HEAD_DIM, 0.0).astype(BF16)


def _head_rms_norm(u, gain, mean_bd):
    ms = _dot((u * u).astype(BF16), mean_bd)
    return u * lax.rsqrt(ms + EPS) * gain


def _rope(u, cos, sin_signed):
    first = (_lane_index(u.shape) & 31) < 16
    partner = jnp.where(first, pltpu.roll(u, LANES - 16, axis=1), pltpu.roll(u, 16, axis=1))
    return u * cos + partner * sin_signed


def _hgrn_lower_bound(lb_ref, layer, direction):
    depth = lb_ref.shape[0] // 2
    rows = [lb_ref[2 * j + direction:2 * j + direction + 1, :] for j in range(depth)]
    m = functools.reduce(jnp.maximum, rows)
    e = [jnp.exp(r - m) for r in rows]
    z = functools.reduce(lambda a, b: a + b, e)
    p = [x / z for x in e]
    csum = functools.reduce(lambda a, b: a + b, p[:layer + 1])
    return csum - p[0]


def _hgrn_forget(z, lb):
    t = jnp.exp(-jnp.abs(z))
    pos = z >= 0.0
    lb_floor = jnp.maximum(lb, LB_FLOOR)
    r = 1.0 / (1.0 + t)
    tr = t * r
    k = (1.0 - lb) * jnp.where(pos, tr, r)
    f = lb_floor + (1.0 - lb_floor) * jnp.where(pos, r, tr)
    return k, jnp.log(f) * LOG2E


GATE_CHUNKS = 6


def _inproj_kernel(x_ref, mod_ref, ctx_mod_ref, g1_ref, w_ref, qg_ref, kg_ref, cos_ref, sin_ref, lb_ref,
                   q_ref, k_ref, v_ref, hv_ref, hk_ref, hlf_ref, hq_ref, hg_ref, four_ref, gate_ref,
                   *, layer):
    tile_rows = x_ref.shape[1]
    groups = [dict(rows=rows) for rows in _row_groups(tile_rows)]
    mean_bd = _head_mean_matrix()
    q_gain = qg_ref[...] * (HEAD_DIM ** -0.5 * LOG2E)
    gate_width = N_BRANCH * D_MODEL // GATE_CHUNKS

    for st in groups:
        rows = st["rows"]
        mod = _group_mods(mod_ref, ctx_mod_ref, tile_rows, rows)
        x = x_ref[0, rows, :]
        ms = jnp.mean(x * x, axis=-1, keepdims=True)
        row_gain = g1_ref[...] * (1.0 + mod[:, D_MODEL:2 * D_MODEL])
        st["h"] = (x * lax.rsqrt(ms + EPS) * row_gain + mod[:, 0:D_MODEL]).astype(BF16)
        st["cos"] = cos_ref[rows, :]
        st["sin"] = sin_ref[rows, :]
        st["low_head"] = _lane_index((rows.stop - rows.start, LANES)) < HEAD_DIM

    def proj(st, seg, width=None):
        a = _OFF[seg]
        b = _OFF[seg + 1] if width is None else a + width
        return _dot(st["h"], w_ref[:, a:b])

    def gate_chunk(st, i):
        a = _OFF[9] + i * gate_width
        gate_ref[0, st["rows"], i * gate_width:(i + 1) * gate_width] = _dot(
            st["h"], w_ref[:, a:a + gate_width]).astype(BF16)

    for st in groups:
        st["kv"] = proj(st, 0, 2 * KV_WIDTH)
        gate_chunk(st, 0)
    for st in groups:
        rows, low_head, kv = st["rows"], st["low_head"], st.pop("kv")
        kr = _rope(_head_rms_norm(kv[:, :KV_WIDTH], kg_ref[...], mean_bd), st["cos"], st["sin"])
        kr_sw = pltpu.roll(kr, HEAD_DIM, axis=1)
        k_ref[0, 0, rows, :] = jnp.where(low_head, kr, kr_sw).astype(BF16)
        k_ref[0, 1, rows, :] = jnp.where(low_head, kr_sw, kr).astype(BF16)
        vr = kv[:, KV_WIDTH:]
        vr_sw = pltpu.roll(vr, HEAD_DIM, axis=1)
        v_ref[0, 0, rows, :] = jnp.where(low_head, vr, vr_sw).astype(BF16)
        v_ref[0, 1, rows, :] = jnp.where(low_head, vr_sw, vr).astype(BF16)

    for st in groups:
        st["q"] = proj(st, 5)
        gate_chunk(st, 1)
    for st in groups:
        qr = st.pop("q")
        for c in range(BRANCH_WIDTH // LANES):
            u = qr[:, c * LANES:(c + 1) * LANES]
            q_ref[0, st["rows"], c * LANES:(c + 1) * LANES] = _rope(
                _head_rms_norm(u, q_gain, mean_bd), st["cos"], st["sin"]).astype(BF16)

    for direction, seg in ((0, 3), (1, 4)):
        lb = _hgrn_lower_bound(lb_ref, layer, direction)
        cols = slice(direction * BRANCH_WIDTH, (direction + 1) * BRANCH_WIDTH)
        for st in groups:
            st["z"] = proj(st, seg)
            gate_chunk(st, 2 + direction)
        for st in groups:
            k, log2f = _hgrn_forget(st.pop("z"), lb)
            hk_ref[0, st["rows"], cols] = k.astype(BF16)
            hlf_ref[0, st["rows"], cols] = log2f
    for st in groups:
        st["qh"] = proj(st, 6)
        gate_chunk(st, 4)
    for st in groups:
        hq_ref[0, st["rows"], :] = _silu(st.pop("qh")).astype(BF16)
    for st in groups:
        rows = st["rows"]
        hv_ref[0, rows, :] = proj(st, 2).astype(BF16)
        hg_ref[0, rows, :] = proj(st, 7).astype(BF16)
        four_ref[0, rows, :] = proj(st, 8).astype(BF16)
        gate_chunk(st, 5)


def _inproj(xs, mods, g1, w_in, qg, kg, cos, sin, lb_logits, *, layer):
    bsz = xs.shape[0]
    row = lambda w: pl.BlockSpec((1, ROW_TILE, w), lambda b, t: (b, t, 0))
    dup = pl.BlockSpec((1, ATTN_KV_HEADS, ROW_TILE, LANES), lambda b, t: (b, 0, t, 0))
    tok = lambda w, dt: jax.ShapeDtypeStruct((bsz, TOKENS, w), dt)
    dup_shape = jax.ShapeDtypeStruct((bsz, ATTN_KV_HEADS, TOKENS, LANES), BF16)
    return pl.pallas_call(
        functools.partial(_inproj_kernel, layer=layer),
        out_shape=(tok(BRANCH_WIDTH, BF16), dup_shape, dup_shape,
                   tok(BRANCH_WIDTH, BF16), tok(2 * BRANCH_WIDTH, BF16), tok(2 * BRANCH_WIDTH, F32),
                   tok(BRANCH_WIDTH, BF16), tok(BRANCH_WIDTH, BF16), tok(BRANCH_WIDTH, BF16),
                   tok(N_BRANCH * D_MODEL, BF16)),
        grid=(bsz, ROW_TILES),
        in_specs=[
            row(D_MODEL),
            *_mod_specs(layer),
            _layer_spec(g1.shape, layer),
            _layer_spec(w_in.shape, layer),
            _layer_spec(qg.shape, layer),
            _layer_spec(kg.shape, layer),
            pl.BlockSpec((ROW_TILE, LANES), lambda b, t: (t, 0)),
            pl.BlockSpec((ROW_TILE, LANES), lambda b, t: (t, 0)),
            _const_spec(lb_logits.shape),
        ],
        out_specs=(row(BRANCH_WIDTH), dup, dup,
                   row(BRANCH_WIDTH), row(2 * BRANCH_WIDTH), row(2 * BRANCH_WIDTH),
                   row(BRANCH_WIDTH), row(BRANCH_WIDTH), row(BRANCH_WIDTH),
                   row(N_BRANCH * D_MODEL)),
        compiler_params=_params(("parallel", "parallel")),
        name="inproj",
    )(xs, mods, mods, g1, w_in, qg, kg, cos, sin, lb_logits)


ATTN_BLOCKS = TOKENS // ATTN_BLOCK
LATENT_ATTN_BLOCKS = SEQ // ATTN_BLOCK
GROUP = ATTN_HEADS // ATTN_KV_HEADS
PAIR_COLS = 2 * ATTN_BLOCK
ATTN_STEP_BLOCKS = 6
assert ATTN_BLOCKS % ATTN_STEP_BLOCKS == 0


def _attend_heads(sink_ref, q_ref, o_ref, query_blocks):
    low_head = _lane_index((ATTN_BLOCK, LANES)) < HEAD_DIM
    zero = jnp.zeros((), BF16)
    odd_cols = lax.broadcasted_iota(jnp.int32, (1, PAIR_COLS), 1) >= ATTN_BLOCK
    tasks = []
    for n, (kv_blocks, masks) in enumerate(query_blocks):
        kv = [kv_blocks(hk) for hk in range(ATTN_KV_HEADS)]
        rows = slice(n * ATTN_BLOCK, (n + 1) * ATTN_BLOCK)
        tasks += [dict(pair=c, rows=rows, kv=kv[(2 * c) // GROUP], masks=masks) for c in range(ATTN_HEADS // 2)]

    def scores_stage(st):
        c = st["pair"]
        tile = q_ref[0, st["rows"], c * LANES:(c + 1) * LANES]
        qs = jnp.concatenate([jnp.where(low_head, tile, zero), jnp.where(low_head, zero, tile)], axis=0)
        scores = [_dot_nt(kb, qs) for kb in st["kv"][0]]
        st["scores"] = [s if mk is None else jnp.where(mk, s, NEG_BIG) for s, mk in zip(scores, st["masks"])]
        st["sink"] = jnp.where(odd_cols, sink_ref[2 * c + 1], sink_ref[2 * c]) * LOG2E

    def max_stage(st):
        m = st["sink"]
        for s in st["scores"]:
            m = jnp.maximum(m, jnp.max(s, axis=0, keepdims=True))
        st["m"] = m

    def values_stage(st):
        denom = jnp.exp2(st["sink"] - st["m"])
        o = None
        for s, vb in zip(st["scores"], st["kv"][1]):
            p = jnp.exp2(s - st["m"])
            denom = denom + jnp.sum(p, axis=0, keepdims=True)
            pv = _dot_tn(vb, p.astype(BF16))
            o = pv if o is None else o + pv
        o = o / denom
        c = st["pair"]
        o_ref[0, st["rows"], c * LANES:(c + 1) * LANES] = jnp.where(
            low_head, o[:, :ATTN_BLOCK].T, o[:, ATTN_BLOCK:].T).astype(BF16)

    stages = (scores_stage, max_stage, values_stage)
    for tick in range(len(tasks) + len(stages) - 1):
        for depth, stage in enumerate(stages):
            if 0 <= tick - depth < len(tasks):
                stage(tasks[tick - depth])


def _attention_kernel(sink_ref, q_ref, k_ref, v_ref, o_ref):
    j2 = lax.broadcasted_iota(jnp.int32, (2 * ATTN_BLOCK, PAIR_COLS), 0)
    t2 = lax.broadcasted_iota(jnp.int32, (2 * ATTN_BLOCK, PAIR_COLS), 1) & (ATTN_BLOCK - 1)
    j1 = lax.broadcasted_iota(jnp.int32, (ATTN_BLOCK, PAIR_COLS), 0)
    t1 = lax.broadcasted_iota(jnp.int32, (ATTN_BLOCK, PAIR_COLS), 1) & (ATTN_BLOCK - 1)

    def query_block(n):
        i = pl.program_id(1) * ATTN_STEP_BLOCKS + n
        prev_start = pl.multiple_of(jnp.maximum(i - 1, 0) * ATTN_BLOCK, ATTN_BLOCK)
        cur_start = pl.multiple_of(i * ATTN_BLOCK, ATTN_BLOCK)
        next_start = pl.multiple_of(jnp.minimum(i + 1, LATENT_ATTN_BLOCKS - 1) * ATTN_BLOCK, ATTN_BLOCK)

        def blocks(ref, hk):
            near = jnp.concatenate([ref[0, hk, pl.ds(prev_start, ATTN_BLOCK), :],
                                    ref[0, hk, pl.ds(cur_start, ATTN_BLOCK), :]], axis=0)
            return near, ref[0, hk, pl.ds(next_start, ATTN_BLOCK), :], ref[0, hk, SEQ:TOKENS, :]

        in_range = jnp.logical_or(j2 >= ATTN_BLOCK, jnp.logical_and(j2 >= t2, i > 0))
        near_valid = jnp.logical_and(in_range, i < LATENT_ATTN_BLOCKS)
        next_valid = jnp.logical_and(j1 <= t1, i < LATENT_ATTN_BLOCKS - 1)
        return (lambda hk: (blocks(k_ref, hk), blocks(v_ref, hk))), (near_valid, next_valid, None)

    _attend_heads(sink_ref, q_ref, o_ref, [query_block(n) for n in range(ATTN_STEP_BLOCKS)])


def _attention(q, k_dup, v_dup, sink):
    bsz = q.shape[0]
    stream = pl.BlockSpec((1, ATTN_KV_HEADS, TOKENS, LANES), lambda b, i: (b, 0, 0, 0))
    step_rows = pl.BlockSpec((1, ATTN_STEP_BLOCKS * ATTN_BLOCK, BRANCH_WIDTH), lambda b, i: (b, i, 0))
    return pl.pallas_call(
        _attention_kernel,
        out_shape=jax.ShapeDtypeStruct((bsz, TOKENS, BRANCH_WIDTH), BF16),
        grid=(bsz, ATTN_BLOCKS // ATTN_STEP_BLOCKS),
        in_specs=[pl.BlockSpec(memory_space=pltpu.SMEM), step_rows, stream, stream],
        out_specs=step_rows,
        compiler_params=_params(("parallel", "arbitrary")),
        name="attention",
    )(sink, q, k_dup, v_dup)


def _cumsum_rows(tri2, x):
    hi = x.astype(BF16)
    lo = (x - hi.astype(F32)).astype(BF16)
    return _dot(tri2, jnp.concatenate([hi, lo], axis=0))


def _hgrn_exact_chunk(reverse, rows, slot, q_ref, k_ref, lf_ref, v_ref, o_ref, old_ref, new_ref, row_ref, acc_ref):
    n = HGRN_CHUNK
    rr = lax.broadcasted_iota(jnp.int32, (n, n), 0)
    cc = lax.broadcasted_iota(jnp.int32, (n, n), 1)
    tri = jnp.where((cc >= rr) if reverse else (cc <= rr), 1.0, 0.0).astype(BF16)
    tri2 = jnp.concatenate([tri, tri], axis=1)
    last_row = 0 if reverse else n - 1
    s_idx = lax.broadcasted_iota(jnp.int32, (n, 1), 0)

    def batch_row(bi, carry):
        g_all = _cumsum_rows(tri2, lf_ref[bi, rows, :])
        for hd in range(HGRN_HEADS):
            ln = slice(hd * HGRN_DK, (hd + 1) * HGRN_DK)
            g = g_all[:, ln]
            q = q_ref[bi, rows, ln].astype(F32)
            k = k_ref[bi, rows, ln].astype(F32)
            v = v_ref[bi, rows, ln]
            vf = v.astype(F32)
            st = old_ref[bi, slot, hd]
            o_inter = _dot_nt((q * jnp.exp2(g)).astype(BF16), st.astype(BF16))
            row_ref[0] = g
            row_ref[1] = q

            def row(t, c):
                g_t = row_ref[0, pl.ds(t, 1), :]
                q_t = row_ref[1, pl.ds(t, 1), :]
                w = jnp.sum(q_t * k * jnp.exp2(jnp.minimum(g_t - g, 0.0)), axis=1, keepdims=True)
                before = (s_idx >= t) if reverse else (s_idx <= t)
                acc_ref[pl.ds(t, 1), :] = jnp.sum(jnp.where(before, w, 0.0) * vf, axis=0, keepdims=True)
                return c

            lax.fori_loop(0, n, row, 0)
            o_ref[bi, rows, ln] = (o_inter + acc_ref[...]).astype(o_ref.dtype)
            g_tot = g[last_row:last_row + 1, :]
            kd = (k * jnp.exp2(g_tot - g)).astype(BF16)
            new_ref[bi, slot, hd] = st * jnp.exp2(g_tot) + _dot_tn(v, kd)
        return carry

    lax.fori_loop(0, HGRN_BATCH, batch_row, 0)


def _hgrn_chunk(directions, old_ref, new_ref, row_ref, acc_ref):
    n = HGRN_CHUNK
    rr = lax.broadcasted_iota(jnp.int32, (n, n), 0)
    cc = lax.broadcasted_iota(jnp.int32, (n, n), 1)

    chains = []
    for slot, (reverse, rows, q_ref, k_ref, lf_ref, v_ref, o_ref) in enumerate(directions):
        causal = (cc >= rr) if reverse else (cc <= rr)
        last_row, mid_row = (0, n // 2) if reverse else (n - 1, n // 2 - 1)
        tri = jnp.where(causal, 1.0, 0.0).astype(BF16)
        tri2 = jnp.concatenate([tri, tri], axis=1)
        for bi in range(HGRN_BATCH):
            g_all = _cumsum_rows(tri2, lf_ref[bi, rows, :])
            for hd in range(HGRN_HEADS):
                ln = slice(hd * HGRN_DK, (hd + 1) * HGRN_DK)
                g = g_all[:, ln]
                chains.append(dict(
                    causal=causal, g=g, g_tot=g[last_row:last_row + 1, :], g_mid=g[mid_row:mid_row + 1, :],
                    q=q_ref[bi, rows, ln], k=k_ref[bi, rows, ln], v=v_ref[bi, rows, ln],
                    o_ref=o_ref, state=(bi, slot, hd), out=(bi, rows, ln)))

    spread = [None] * len(directions)
    for ch in chains:
        d = ch["g"] - ch["g_mid"]
        slot = ch["state"][1]
        spread[slot] = jnp.abs(d) if spread[slot] is None else jnp.maximum(spread[slot], jnp.abs(d))
        ch["qa"] = ch["q"] * jnp.exp2(jnp.minimum(d, EXP2_CLAMP)).astype(BF16)
        ch["kb"] = ch["k"] * jnp.exp2(jnp.minimum(-d, EXP2_CLAMP)).astype(BF16)
        ch["st_mid"] = old_ref[ch["state"]] * jnp.exp2(ch["g_mid"])
    for ch in chains:
        ch["a"] = _dot_nt(ch["qa"], ch["kb"])
        ch["o_inter"] = _dot_nt(ch["qa"], ch["st_mid"].astype(BF16))
        ch["u"] = _dot_tn(ch["v"], ch["kb"])
    for ch in chains:
        a = jnp.where(ch["causal"], ch["a"], 0.0).astype(BF16)
        ch["o_ref"][ch["out"]] = (ch["o_inter"] + _dot(a, ch["v"])).astype(ch["o_ref"].dtype)
        new_ref[ch["state"]] = (ch["st_mid"] + ch["u"]) * jnp.exp2(ch["g_tot"] - ch["g_mid"])

    @pl.when(jnp.max(functools.reduce(jnp.maximum, spread)) > EXP2_CLAMP)
    def _():
        for slot, (reverse, rows, q_ref, k_ref, lf_ref, v_ref, o_ref) in enumerate(directions):
            _hgrn_exact_chunk(reverse, rows, slot, q_ref, k_ref, lf_ref, v_ref, o_ref, old_ref, new_ref,
                              row_ref, acc_ref)


def _hgrn_kernel(qf_ref, kf_ref, lf_ref, vf_ref, qb_ref, kb_ref, lb_ref, vb_ref, of_ref, ob_ref,
                 state_ref, row_ref, acc_ref):
    @pl.when(pl.program_id(1) == 0)
    def _():
        state_ref[0] = jnp.zeros(state_ref.shape[1:], F32)

    first, second = slice(0, HGRN_CHUNK), slice(HGRN_CHUNK, 2 * HGRN_CHUNK)
    for sub in range(2):
        directions = ((False, (first, second)[sub], qf_ref, kf_ref, lf_ref, vf_ref, of_ref),
                      (True, (second, first)[sub], qb_ref, kb_ref, lb_ref, vb_ref, ob_ref))
        _hgrn_chunk(directions, state_ref.at[sub], state_ref.at[1 - sub], row_ref, acc_ref)


HGRN_STEP_ROWS = 2 * HGRN_CHUNK
HGRN_STEPS = TOKENS // HGRN_STEP_ROWS
HGRN_CTX_STEPS = CTX_LEN // HGRN_STEP_ROWS


def _hgrn(hq, hk, hlf, hv):
    bsz = hq.shape[0]
    assert bsz % HGRN_BATCH == 0
    lat = HGRN_STEPS - HGRN_CTX_STEPS

    def fwd(c):
        return jnp.where(c < HGRN_CTX_STEPS, lat + c, c - HGRN_CTX_STEPS)

    def bwd(c):
        return HGRN_STEPS - 1 - c

    def spec(order, lane_block):
        return pl.BlockSpec((HGRN_BATCH, HGRN_STEP_ROWS, BRANCH_WIDTH), lambda b, c: (b, order(c), lane_block))

    out = jax.ShapeDtypeStruct((bsz, TOKENS, BRANCH_WIDTH), BF16)
    return pl.pallas_call(
        _hgrn_kernel,
        out_shape=(out, out),
        grid=(bsz // HGRN_BATCH, HGRN_STEPS),
        in_specs=[spec(fwd, 0), spec(fwd, 0), spec(fwd, 0), spec(fwd, 0),
                  spec(bwd, 0), spec(bwd, 1), spec(bwd, 1), spec(bwd, 0)],
        out_specs=(spec(fwd, 0), spec(bwd, 0)),
        scratch_shapes=[pltpu.VMEM((2, HGRN_BATCH, 2, HGRN_HEADS, HGRN_DK, HGRN_DK), F32),
                        pltpu.VMEM((2, HGRN_CHUNK, HGRN_DK), F32), pltpu.VMEM((HGRN_CHUNK, HGRN_DK), F32)],
        compiler_params=_params(("parallel", "arbitrary")),
        name="hgrn",
    )(hq, hk, hlf, hv, hq, hk, hlf, hv)


FOURIER_ROW_TILE = 768
FOURIER_TILES = TOKENS // FOURIER_ROW_TILE
FOURIER_LAST_LATENT = SEQ - (FOURIER_TILES - 1) * FOURIER_ROW_TILE
assert FOURIER_TILES * FOURIER_ROW_TILE == TOKENS and FOURIER_LAST_LATENT + CTX_LEN == FOURIER_ROW_TILE


HALF_SEQ = SEQ // 2


def _fourier_kernel(x_ref, xc_ref, wc_ref, cs_ref, csc_ref, o_ref, uv_ref, nyq_ref):
    t = pl.program_id(1)
    last = FOURIER_TILES - 1
    latent_scale = 1.0 / math.sqrt(SEQ * FOURIER_GROUP_DIM)
    blk = LANES

    @pl.when(t == 0)
    def _():
        rr = lax.broadcasted_iota(jnp.int32, (blk, blk), 0)
        cc = lax.broadcasted_iota(jnp.int32, (blk, blk), 1)
        flip = jnp.where(rr + cc == blk, 1.0, 0.0).astype(BF16)
        first_row = lax.broadcasted_iota(jnp.int32, (blk, 1), 0) == 0
        n_blk = HALF_SEQ // blk
        cos_c = wc_ref[:, :BRANCH_WIDTH]
        sin_c = wc_ref[:, BRANCH_WIDTH:]
        for a in range(n_blk):
            src = HALF_SEQ + (n_blk - 1 - a) * blk
            z = _dot(flip, x_ref[0, src:src + blk, :])
            if a > 0:
                z = jnp.where(first_row, x_ref[0, src + blk:src + blk + 16, :].astype(F32)[0:1, :], z)
            lo = x_ref[0, a * blk:(a + 1) * blk, :].astype(F32)
            uv_ref[a * blk:(a + 1) * blk, :] = _dot((lo + z).astype(BF16), cos_c).astype(BF16)
            uv_ref[HALF_SEQ + a * blk:HALF_SEQ + (a + 1) * blk, :] = _dot((lo - z).astype(BF16), sin_c).astype(BF16)
        nyq_ref[...] = _dot(x_ref[0, HALF_SEQ:HALF_SEQ + nyq_ref.shape[0], :], cos_c)

    def latent_rows(r0, n):
        sign = 1.0 - 2.0 * ((r0 + lax.broadcasted_iota(jnp.int32, (n, 1), 0)) & 1).astype(F32)
        return (_dot(cs_ref[pl.ds(r0, n), :], uv_ref[...]) + sign * nyq_ref[0:1, :]) * latent_scale

    @pl.when(t < last)
    def _():
        r0 = pl.multiple_of(t * FOURIER_ROW_TILE, FOURIER_ROW_TILE)
        o_ref[0] = latent_rows(r0, FOURIER_ROW_TILE).astype(BF16)

    @pl.when(t == last)
    def _():
        o_ref[0, 0:FOURIER_LAST_LATENT, :] = latent_rows(last * FOURIER_ROW_TILE, FOURIER_LAST_LATENT).astype(BF16)
        uv = _dot(xc_ref[0], wc_ref[...])
        uvc = jnp.concatenate([uv[:, :BRANCH_WIDTH], uv[:, BRANCH_WIDTH:]], axis=0).astype(BF16)
        yc = _dot(csc_ref[...], uvc)
        o_ref[0, FOURIER_LAST_LATENT:FOURIER_ROW_TILE, :] = (
            yc * (1.0 / math.sqrt(CTX_LEN * FOURIER_GROUP_DIM))).astype(BF16)


def _fourier(four, wc, cs_lat, cs_ctx):
    bsz = four.shape[0]
    return pl.pallas_call(
        _fourier_kernel,
        out_shape=jax.ShapeDtypeStruct((bsz, TOKENS, BRANCH_WIDTH), BF16),
        grid=(bsz, FOURIER_TILES),
        in_specs=[pl.BlockSpec((1, SEQ, BRANCH_WIDTH), lambda b, t: (b, 0, 0)),
                  pl.BlockSpec((1, CTX_LEN, BRANCH_WIDTH), lambda b, t: (b, SEQ // CTX_LEN, 0)),
                  _const_spec(wc.shape), _const_spec(cs_lat.shape), _const_spec(cs_ctx.shape)],
        out_specs=pl.BlockSpec((1, FOURIER_ROW_TILE, BRANCH_WIDTH), lambda b, t: (b, t, 0)),
        scratch_shapes=[pltpu.VMEM((2 * HALF_SEQ, BRANCH_WIDTH), BF16), pltpu.VMEM((16, BRANCH_WIDTH), F32)],
        compiler_params=_params(("parallel", "arbitrary")),
        name="fourier",
    )(four, four, wc, cs_lat, cs_ctx)


def _dft_tables(n, split, n_cols):
    col = jnp.arange(n_cols, dtype=jnp.int32)

    def trig(row_factor):
        ang = ((row_factor[:, None] * col[None, :]) % n).astype(F32) * (2.0 * math.pi / n)
        return jnp.cos(ang), jnp.sin(ang)

    cos_a, sin_a = trig(split * jnp.arange(n // split, dtype=jnp.int32))
    cos_b, sin_b = trig(jnp.arange(split, dtype=jnp.int32))
    cos = cos_a[:, None, :] * cos_b[None, :, :] - sin_a[:, None, :] * sin_b[None, :, :]
    sin = sin_a[:, None, :] * cos_b[None, :, :] + cos_a[:, None, :] * sin_b[None, :, :]
    return jnp.concatenate([cos.reshape(n, n_cols), -sin.reshape(n, n_cols)], axis=1).astype(BF16)


def _channel_dft_table():
    c = jnp.arange(BRANCH_WIDTH, dtype=jnp.int32)
    same = (c[:, None] // FOURIER_GROUP_DIM) == (c[None, :] // FOURIER_GROUP_DIM)
    ang = (((c[:, None] % FOURIER_GROUP_DIM) * (c[None, :] % FOURIER_GROUP_DIM)) % FOURIER_GROUP_DIM
           ).astype(F32) * (2.0 * math.pi / FOURIER_GROUP_DIM)
    cos = jnp.where(same, jnp.cos(ang), 0.0)
    sin = jnp.where(same, jnp.sin(ang), 0.0)
    return jnp.concatenate([cos, sin], axis=1).astype(BF16)


FF_CHUNK = 1024


def _rms(x):
    return x * lax.rsqrt(jnp.mean(x * x, axis=-1, keepdims=True) + EPS)


def _merge_mlp_kernel(x_ref, mod_ref, ctx_mod_ref, of_ref, oa_ref, hf_ref, hb_ref, hg_ref, gate_ref,
                      hng_ref, g2_ref, wb_ref, wo_ref, w1_ref, w2_ref, out_ref):
    d = D_MODEL
    tile_rows = x_ref.shape[1]
    groups = [dict(rows=rows) for rows in _row_groups(tile_rows)]

    def mix_stage(st):
        rows = st["rows"]
        mods = _group_mods(mod_ref, ctx_mod_ref, tile_rows, rows)
        mod = st["mod"] = lambda i: mods[:, i * d:(i + 1) * d]
        o_h = hf_ref[0, rows, :].astype(F32) + hb_ref[0, rows, :].astype(F32)
        o_n = jnp.concatenate(
            [_rms(o_h[:, hd * HGRN_DK:(hd + 1) * HGRN_DK]) * hng_ref[...] for hd in range(HGRN_HEADS)], axis=-1)
        o_hr = (o_n * _silu(hg_ref[0, rows, :].astype(F32))).astype(BF16)
        branches = (of_ref[0, rows, :], oa_ref[0, rows, :], o_hr)
        mix = None
        for n, br in enumerate(branches):
            term = _sigmoid(gate_ref[0, rows, n * d:(n + 1) * d].astype(F32)) * _dot(br, wb_ref[n])
            mix = term if mix is None else mix + term
        y = _dot(mix.astype(BF16), wo_ref[...])
        st["x1"] = x1 = x_ref[0, rows, :] + mod(2) * y
        st["h2"] = (_rms(x1) * (g2_ref[...] * (1.0 + mod(4))) + mod(3)).astype(BF16)

    def mlp_stage(st):
        ff = None
        for c in range(D_FF // FF_CHUNK):
            cols = slice(c * FF_CHUNK, (c + 1) * FF_CHUNK)
            u = jnp.square(jnp.maximum(_dot(st["h2"], w1_ref[:, cols]), 0.0)).astype(BF16)
            part = _dot(u, w2_ref[cols, :])
            ff = part if ff is None else ff + part
        out_ref[0, st["rows"], :] = st["x1"] + st["mod"](5) * ff

    for stage in (mix_stage, mlp_stage):
        for st in groups:
            stage(st)


def _merge_mlp(xs, mods, o_four, o_attn, o_hf, o_hb, hg, gate, hng, g2, wb, wo, w1, w2, *, layer, latent_only):
    bsz = xs.shape[0]
    rows = LATENT_ROW_TILE if latent_only else ROW_TILE
    n_rows = SEQ if latent_only else TOKENS
    row = lambda w: pl.BlockSpec((1, rows, w), lambda b, t: (b, t, 0))
    return pl.pallas_call(
        _merge_mlp_kernel,
        out_shape=jax.ShapeDtypeStruct((bsz, n_rows, D_MODEL), F32),
        grid=(bsz, n_rows // rows),
        in_specs=[row(D_MODEL),
                  *_mod_specs(layer),
                  row(BRANCH_WIDTH), row(BRANCH_WIDTH), row(BRANCH_WIDTH), row(BRANCH_WIDTH),
                  row(BRANCH_WIDTH), row(N_BRANCH * D_MODEL),
                  *[_layer_spec(a.shape, layer) for a in (hng, g2, wb, wo, w1, w2)]],
        out_specs=row(D_MODEL),
        compiler_params=_params(("parallel", "parallel")),
        name="merge_mlp",
    )(xs, mods, mods, o_four, o_attn, o_hf, o_hb, hg, gate, hng, g2, wb, wo, w1, w2)


def _rope_tables():
    pos = jnp.arange(SEQ)
    row = (pos // GRID_W).astype(F32)
    col = (pos % GRID_W).astype(F32)
    axis_dim = HEAD_DIM // 2
    half = axis_dim // 2
    inv_freq = ROPE_THETA ** (-jnp.arange(0, axis_dim, 2, dtype=F32) / axis_dim)
    ang_r = row[:, None] * inv_freq
    ang_c = col[:, None] * inv_freq
    ang = jnp.concatenate([ang_r, ang_r, ang_c, ang_c], axis=1)
    sign = jnp.tile(jnp.concatenate([-jnp.ones(half, F32), jnp.ones(half, F32)]), 2)
    cos = jnp.concatenate([jnp.cos(ang), jnp.ones((CTX_LEN, HEAD_DIM), F32)], axis=0)
    sin = jnp.concatenate([jnp.sin(ang) * sign, jnp.zeros((CTX_LEN, HEAD_DIM), F32)], axis=0)
    return jnp.tile(cos, (1, 2)), jnp.tile(sin, (1, 2))


def kernel(x, c, ctx, c_ctx, w_mod, b_mod, norm1_g, norm2_g, w_in, q_norm_g, k_norm_g, attn_sink,
           hgrn_lb_logits, hgrn_norm_g, w_branch, w_out, w_ff1, w_ff2):
    bsz, n_tok, d = x.shape
    depth = w_mod.shape[0]
    assert (n_tok, d, ctx.shape[1]) == (SEQ, D_MODEL, CTX_LEN) and bsz <= MOD_ROWS // 2

    xs = jnp.concatenate([x, ctx], axis=1)
    cond = jnp.zeros((MOD_ROWS, d), F32).at[:bsz].set(c).at[MOD_ROWS // 2].set(c_ctx)
    mods = _modulation(cond, w_mod, b_mod).reshape(depth, MOD_ROWS, 1, 6 * d)

    cos, sin = _rope_tables()
    wc = _channel_dft_table()
    cs_lat = _dft_tables(SEQ, 32, HALF_SEQ)
    cs_ctx = _dft_tables(CTX_LEN, 16, CTX_LEN)
    lb_logits = hgrn_lb_logits.reshape(depth * 2, BRANCH_WIDTH)

    g1 = norm1_g.reshape(depth, 1, d)
    g2 = norm2_g.reshape(depth, 1, d)
    qg = jnp.tile(q_norm_g, (1, 2)).reshape(depth, 1, LANES)
    kg = jnp.tile(k_norm_g, (1, 2)).reshape(depth, 1, LANES)
    hng = hgrn_norm_g.reshape(depth, 1, HGRN_DK)
    w_in, w_branch, w_out, w_ff1, w_ff2 = (w.astype(BF16) for w in (w_in, w_branch, w_out, w_ff1, w_ff2))

    for l in range(depth):
        q, k_dup, v_dup, hv, hk, hlf, hq, hg, four, gate = _inproj(
            xs, mods, g1, w_in, qg, kg, cos, sin, lb_logits, layer=l)
        o_attn = _attention(q, k_dup, v_dup, attn_sink[l])
        o_hf, o_hb = _hgrn(hq, hk, hlf, hv)
        o_four = _fourier(four, wc, cs_lat, cs_ctx)
        xs = _merge_mlp(xs, mods, o_four, o_attn, o_hf, o_hb, hg, gate, hng, g2,
                        w_branch, w_out, w_ff1, w_ff2, layer=l, latent_only=(l == depth - 1))
    return xs
```

```python
import functools
import math

import jax
import jax.numpy as jnp
from jax import lax
from jax.experimental import pallas as pl
from jax.experimental.pallas import tpu as pltpu

F32 = jnp.float32
BF16 = jnp.bfloat16

D_MODEL = 1024
SEQ = 2048
CTX_LEN = 256
TOKENS = SEQ + CTX_LEN
GRID_W = 64
EPS = 1e-6
NEG_BIG = -1e30
LB_FLOOR = 1e-30
ROPE_THETA = 10000.0

BRANCH_WIDTH = D_MODEL // 2
HEAD_DIM = 64
ATTN_HEADS = BRANCH_WIDTH // HEAD_DIM
ATTN_KV_HEADS = ATTN_HEADS // 4
KV_WIDTH = ATTN_KV_HEADS * HEAD_DIM
ATTN_BLOCK = 128
FOURIER_GROUPS = 4
FOURIER_GROUP_DIM = BRANCH_WIDTH // FOURIER_GROUPS
HGRN_HEADS = 4
HGRN_DK = BRANCH_WIDTH // HGRN_HEADS
D_FF = 4 * D_MODEL
N_BRANCH = 3

_SEG = (KV_WIDTH, KV_WIDTH, BRANCH_WIDTH, BRANCH_WIDTH, BRANCH_WIDTH,
        BRANCH_WIDTH, BRANCH_WIDTH, BRANCH_WIDTH, BRANCH_WIDTH, N_BRANCH * D_MODEL)
_OFF = [sum(_SEG[:i]) for i in range(len(_SEG) + 1)]

LANES = 128
VMEM_LIMIT_BYTES = 56 * 1024 * 1024

ROW_TILE = 576
ROW_TILES = TOKENS // ROW_TILE
LATENT_ROW_TILE = 512
assert ROW_TILES * ROW_TILE == TOKENS and SEQ % LATENT_ROW_TILE == 0
HGRN_CHUNK = 64
HGRN_BATCH = 8
EXP2_CLAMP = 100.0
LOG2E = 1.0 / math.log(2.0)
MOD_ROWS = 16


def _const_spec(shape):
    nd = len(shape)
    return pl.BlockSpec(shape, lambda *_: (0,) * nd, pipeline_mode=pl.Buffered(1))


def _layer_spec(stacked_shape, layer):
    rest = tuple(stacked_shape[1:])
    return pl.BlockSpec((None,) + rest, lambda *_: (layer,) + (0,) * len(rest), pipeline_mode=pl.Buffered(1))


def _params(semantics, flags=None):
    return pltpu.CompilerParams(dimension_semantics=semantics, vmem_limit_bytes=VMEM_LIMIT_BYTES,
                                flags=flags)


def _dot(a, b):
    return jnp.dot(a, b, preferred_element_type=F32)


def _dot_nt(a, b):
    return lax.dot_general(a, b, (((1,), (1,)), ((), ())), preferred_element_type=F32)


def _dot_tn(a, b):
    return lax.dot_general(a, b, (((0,), (0,)), ((), ())), preferred_element_type=F32)


def _sigmoid(x):
    return 0.5 * jnp.tanh(0.5 * x) + 0.5


def _silu(x):
    return x * _sigmoid(x)


def _mod_specs(layer):
    one = lambda index: pl.BlockSpec((None, None, 1, 6 * D_MODEL), index)
    return [one(lambda b, t: (layer, b, 0, 0)), one(lambda b, t: (layer, MOD_ROWS // 2, 0, 0))]


def _row_groups(tile_rows):
    split = SEQ % tile_rows or tile_rows // 2
    return slice(0, split), slice(split, tile_rows)


def _group_mods(mod_ref, ctx_mod_ref, tile_rows, rows):
    first_row = pl.program_id(1) * tile_rows + rows.start
    return jnp.where(first_row >= SEQ, ctx_mod_ref[...], mod_ref[...])


MOD_COL_TILE = 1536


def _modulation_kernel(cond_ref, w_ref, b_ref, out_ref):
    a = _silu(cond_ref[...]).astype(BF16)
    out_ref[...] = _dot(a, w_ref[...].astype(BF16)) + b_ref[...]


def _modulation(cond, w_mod, b_mod):
    depth = w_mod.shape[0]
    n = w_mod.shape[2]
    return pl.pallas_call(
        _modulation_kernel,
        out_shape=jax.ShapeDtypeStruct((depth, MOD_ROWS, n), F32),
        grid=(depth, n // MOD_COL_TILE),
        in_specs=[
            pl.BlockSpec((MOD_ROWS, D_MODEL), lambda l, j: (0, 0)),
            pl.BlockSpec((None, D_MODEL, MOD_COL_TILE), lambda l, j: (l, 0, j)),
            pl.BlockSpec((None, 1, MOD_COL_TILE), lambda l, j: (l, 0, j)),
        ],
        out_specs=pl.BlockSpec((None, MOD_ROWS, MOD_COL_TILE), lambda l, j: (l, 0, j)),
        compiler_params=_params(("parallel", "parallel")),
        name="modulation",
    )(cond, w_mod, b_mod.reshape(depth, 1, n))


def _lane_index(shape):
    return lax.broadcasted_iota(jnp.int32, shape, len(shape) - 1)


def _head_mean_matrix():
    shift = HEAD_DIM.bit_length() - 1
    r = lax.broadcasted_iota(jnp.int32, (LANES, LANES), 0) >> shift
    c = lax.broadcasted_iota(jnp.int32, (LANES, LANES), 1) >> shift
    return jnp.where(r == c, 1.0 / HEAD_DIM, 0.0).astype(BF16)


def _head_rms_norm(u, gain, mean_bd):
    ms = _dot((u * u).astype(BF16), mean_bd)
    return u * lax.rsqrt(ms + EPS) * gain


def _rope(u, cos, sin_signed):
    first = (_lane_index(u.shape) & 31) < 16
    partner = jnp.where(first, pltpu.roll(u, LANES - 16, axis=1), pltpu.roll(u, 16, axis=1))
    return u * cos + partner * sin_signed


def _hgrn_lower_bound(lb_ref, layer, direction):
    depth = lb_ref.shape[0] // 2
    rows = [lb_ref[2 * j + direction:2 * j + direction + 1, :] for j in range(depth)]
    m = functools.reduce(jnp.maximum, rows)
    e = [jnp.exp(r - m) for r in rows]
    z = functools.reduce(lambda a, b: a + b, e)
    p = [x / z for x in e]
    csum = functools.reduce(lambda a, b: a + b, p[:layer + 1])
    return csum - p[0]


def _hgrn_forget(z, lb):
    t = jnp.exp(-jnp.abs(z))
    pos = z >= 0.0
    lb_floor = jnp.maximum(lb, LB_FLOOR)
    r = 1.0 / (1.0 + t)
    tr = t * r
    k = (1.0 - lb) * jnp.where(pos, tr, r)
    f = lb_floor + (1.0 - lb_floor) * jnp.where(pos, r, tr)
    return k, jnp.log(f) * LOG2E


GATE_CHUNKS = 6


def _inproj_kernel(x_ref, mod_ref, ctx_mod_ref, g1_ref, w_ref, qg_ref, kg_ref, cos_ref, sin_ref, lb_ref,
                   q_ref, k_ref, v_ref, hv_ref, hk_ref, hlf_ref, hq_ref, hg_ref, four_ref, gate_ref,
                   *, layer):
    tile_rows = x_ref.shape[1]
    groups = [dict(rows=rows) for rows in _row_groups(tile_rows)]
    mean_bd = _head_mean_matrix()
    q_gain = qg_ref[...] * (HEAD_DIM ** -0.5 * LOG2E)
    gate_width = N_BRANCH * D_MODEL // GATE_CHUNKS

    for st in groups:
        rows = st["rows"]
        mod = _group_mods(mod_ref, ctx_mod_ref, tile_rows, rows)
        x = x_ref[0, rows, :]
        ms = jnp.mean(x * x, axis=-1, keepdims=True)
        row_gain = g1_ref[...] * (1.0 + mod[:, D_MODEL:2 * D_MODEL])
        st["h"] = (x * lax.rsqrt(ms + EPS) * row_gain + mod[:, 0:D_MODEL]).astype(BF16)
        st["cos"] = cos_ref[rows, :]
        st["sin"] = sin_ref[rows, :]
        st["low_head"] = _lane_index((rows.stop - rows.start, LANES)) < HEAD_DIM

    def proj(st, seg, width=None):
        a = _OFF[seg]
        b = _OFF[seg + 1] if width is None else a + width
        return _dot(st["h"], w_ref[:, a:b])

    def gate_chunk(st, i):
        a = _OFF[9] + i * gate_width
        gate_ref[0, st["rows"], i * gate_width:(i + 1) * gate_width] = _dot(
            st["h"], w_ref[:, a:a + gate_width]).astype(BF16)

    for st in groups:
        st["kv"] = proj(st, 0, 2 * KV_WIDTH)
        gate_chunk(st, 0)
    for st in groups:
        rows, low_head, kv = st["rows"], st["low_head"], st.pop("kv")
        kr = _rope(_head_rms_norm(kv[:, :KV_WIDTH], kg_ref[...], mean_bd), st["cos"], st["sin"])
        kr_sw = pltpu.roll(kr, HEAD_DIM, axis=1)
        k_ref[0, 0, rows, :] = jnp.where(low_head, kr, kr_sw).astype(BF16)
        k_ref[0, 1, rows, :] = jnp.where(low_head, kr_sw, kr).astype(BF16)
        vr = kv[:, KV_WIDTH:]
        vr_sw = pltpu.roll(vr, HEAD_DIM, axis=1)
        v_ref[0, 0, rows, :] = jnp.where(low_head, vr, vr_sw).astype(BF16)
        v_ref[0, 1, rows, :] = jnp.where(low_head, vr_sw, vr).astype(BF16)

    for st in groups:
        st["q"] = proj(st, 5)
        gate_chunk(st, 1)
    for st in groups:
        qr = st.pop("q")
        for c in range(BRANCH_WIDTH // LANES):
            u = qr[:, c * LANES:(c + 1) * LANES]
            q_ref[0, st["rows"], c * LANES:(c + 1) * LANES] = _rope(
                _head_rms_norm(u, q_gain, mean_bd), st["cos"], st["sin"]).astype(BF16)

    for direction, seg in ((0, 3), (1, 4)):
        lb = _hgrn_lower_bound(lb_ref, layer, direction)
        cols = slice(direction * BRANCH_WIDTH, (direction + 1) * BRANCH_WIDTH)
        for st in groups:
            st["z"] = proj(st, seg)
            gate_chunk(st, 2 + direction)
        for st in groups:
            k, log2f = _hgrn_forget(st.pop("z"), lb)
            hk_ref[0, st["rows"], cols] = k.astype(BF16)
            hlf_ref[0, st["rows"], cols] = log2f
    for st in groups:
        st["qh"] = proj(st, 6)
        gate_chunk(st, 4)
    for st in groups:
        hq_ref[0, st["rows"], :] = _silu(st.pop("qh")).astype(BF16)
    for st in groups:
        rows = st["rows"]
        hv_ref[0, rows, :] = proj(st, 2).astype(BF16)
        hg_ref[0, rows, :] = proj(st, 7).astype(BF16)
        four_ref[0, rows, :] = proj(st, 8).astype(BF16)
        gate_chunk(st, 5)


def _inproj(xs, mods, g1, w_in, qg, kg, cos, sin, lb_logits, *, layer):
    bsz = xs.shape[0]
    row = lambda w: pl.BlockSpec((1, ROW_TILE, w), lambda b, t: (b, t, 0))
    dup = pl.BlockSpec((1, ATTN_KV_HEADS, ROW_TILE, LANES), lambda b, t: (b, 0, t, 0))
    tok = lambda w, dt: jax.ShapeDtypeStruct((bsz, TOKENS, w), dt)
    dup_shape = jax.ShapeDtypeStruct((bsz, ATTN_KV_HEADS, TOKENS, LANES), BF16)
    return pl.pallas_call(
        functools.partial(_inproj_kernel, layer=layer),
        out_shape=(tok(BRANCH_WIDTH, BF16), dup_shape, dup_shape,
                   tok(BRANCH_WIDTH, BF16), tok(2 * BRANCH_WIDTH, BF16), tok(2 * BRANCH_WIDTH, F32),
                   tok(BRANCH_WIDTH, BF16), tok(BRANCH_WIDTH, BF16), tok(BRANCH_WIDTH, BF16),
                   tok(N_BRANCH * D_MODEL, BF16)),
        grid=(bsz, ROW_TILES),
        in_specs=[
            row(D_MODEL),
            *_mod_specs(layer),
            _layer_spec(g1.shape, layer),
            _layer_spec(w_in.shape, layer),
            _layer_spec(qg.shape, layer),
            _layer_spec(kg.shape, layer),
            pl.BlockSpec((ROW_TILE, LANES), lambda b, t: (t, 0)),
            pl.BlockSpec((ROW_TILE, LANES), lambda b, t: (t, 0)),
            _const_spec(lb_logits.shape),
        ],
        out_specs=(row(BRANCH_WIDTH), dup, dup,
                   row(BRANCH_WIDTH), row(2 * BRANCH_WIDTH), row(2 * BRANCH_WIDTH),
                   row(BRANCH_WIDTH), row(BRANCH_WIDTH), row(BRANCH_WIDTH),
                   row(N_BRANCH * D_MODEL)),
        compiler_params=_params(("parallel", "parallel")),
        name="inproj",
    )(xs, mods, mods, g1, w_in, qg, kg, cos, sin, lb_logits)


ATTN_BLOCKS = TOKENS // ATTN_BLOCK
LATENT_ATTN_BLOCKS = SEQ // ATTN_BLOCK
GROUP = ATTN_HEADS // ATTN_KV_HEADS
PAIR_COLS = 2 * ATTN_BLOCK
ATTN_STEP_BLOCKS = 9
assert ATTN_BLOCKS % ATTN_STEP_BLOCKS == 0


def _attend_heads(sink_ref, q_ref, o_ref, query_blocks):
    low_head = _lane_index((ATTN_BLOCK, LANES)) < HEAD_DIM
    zero = jnp.zeros((), BF16)
    odd_cols = lax.broadcasted_iota(jnp.int32, (1, PAIR_COLS), 1) >= ATTN_BLOCK
    tasks = []
    for n, (kv_blocks, masks) in enumerate(query_blocks):
        kv = [kv_blocks(hk) for hk in range(ATTN_KV_HEADS)]
        rows = slice(n * ATTN_BLOCK, (n + 1) * ATTN_BLOCK)
        tasks += [dict(pair=c, rows=rows, kv=kv[(2 * c) // GROUP], masks=masks) for c in range(ATTN_HEADS // 2)]

    def scores_stage(st):
        c = st["pair"]
        tile = q_ref[0, st["rows"], c * LANES:(c + 1) * LANES]
        qs = jnp.concatenate([jnp.where(low_head, tile, zero), jnp.where(low_head, zero, tile)], axis=0)
        scores = [_dot_nt(kb, qs) for kb in st["kv"][0]]
        st["scores"] = [s if mk is None else jnp.where(mk, s, NEG_BIG) for s, mk in zip(scores, st["masks"])]
        st["sink"] = jnp.where(odd_cols, sink_ref[2 * c + 1], sink_ref[2 * c]) * LOG2E

    def max_stage(st):
        m = st["sink"]
        for s in st["scores"]:
            m = jnp.maximum(m, jnp.max(s, axis=0, keepdims=True))
        st["m"] = m

    def values_stage(st):
        denom = jnp.exp2(st["sink"] - st["m"])
        o = None
        for s, vb in zip(st["scores"], st["kv"][1]):
            p = jnp.exp2(s - st["m"])
            denom = denom + jnp.sum(p, axis=0, keepdims=True)
            pv = _dot_tn(vb, p.astype(BF16))
            o = pv if o is None else o + pv
        o = o / denom
        c = st["pair"]
        o_ref[0, st["rows"], c * LANES:(c + 1) * LANES] = jnp.where(
            low_head, o[:, :ATTN_BLOCK].T, o[:, ATTN_BLOCK:].T).astype(BF16)

    stages = (scores_stage, max_stage, values_stage)
    for tick in range(len(tasks) + len(stages) - 1):
        for depth, stage in enumerate(stages):
            if 0 <= tick - depth < len(tasks):
                stage(tasks[tick - depth])


def _attention_kernel(sink_ref, q_ref, k_ref, v_ref, o_ref):
    j2 = lax.broadcasted_iota(jnp.int32, (2 * ATTN_BLOCK, PAIR_COLS), 0)
    t2 = lax.broadcasted_iota(jnp.int32, (2 * ATTN_BLOCK, PAIR_COLS), 1) & (ATTN_BLOCK - 1)
    j1 = lax.broadcasted_iota(jnp.int32, (ATTN_BLOCK, PAIR_COLS), 0)
    t1 = lax.broadcasted_iota(jnp.int32, (ATTN_BLOCK, PAIR_COLS), 1) & (ATTN_BLOCK - 1)

    def query_block(n):
        i = pl.program_id(1) * ATTN_STEP_BLOCKS + n
        prev_start = pl.multiple_of(jnp.maximum(i - 1, 0) * ATTN_BLOCK, ATTN_BLOCK)
        cur_start = pl.multiple_of(i * ATTN_BLOCK, ATTN_BLOCK)
        next_start = pl.multiple_of(jnp.minimum(i + 1, LATENT_ATTN_BLOCKS - 1) * ATTN_BLOCK, ATTN_BLOCK)

        def blocks(ref, hk):
            near = jnp.concatenate([ref[0, hk, pl.ds(prev_start, ATTN_BLOCK), :],
                                    ref[0, hk, pl.ds(cur_start, ATTN_BLOCK), :]], axis=0)
            return near, ref[0, hk, pl.ds(next_start, ATTN_BLOCK), :], ref[0, hk, SEQ:TOKENS, :]

        in_range = jnp.logical_or(j2 >= ATTN_BLOCK, jnp.logical_and(j2 >= t2, i > 0))
        near_valid = jnp.logical_and(in_range, i < LATENT_ATTN_BLOCKS)
        next_valid = jnp.logical_and(j1 <= t1, i < LATENT_ATTN_BLOCKS - 1)
        return (lambda hk: (blocks(k_ref, hk), blocks(v_ref, hk))), (near_valid, next_valid, None)

    _attend_heads(sink_ref, q_ref, o_ref, [query_block(n) for n in range(ATTN_STEP_BLOCKS)])


def _attention(q, k_dup, v_dup, sink):
    bsz = q.shape[0]
    stream = pl.BlockSpec((1, ATTN_KV_HEADS, TOKENS, LANES), lambda b, i: (b, 0, 0, 0))
    step_rows = pl.BlockSpec((1, ATTN_STEP_BLOCKS * ATTN_BLOCK, BRANCH_WIDTH), lambda b, i: (b, i, 0))
    return pl.pallas_call(
        _attention_kernel,
        out_shape=jax.ShapeDtypeStruct((bsz, TOKENS, BRANCH_WIDTH), BF16),
        grid=(bsz, ATTN_BLOCKS // ATTN_STEP_BLOCKS),
        in_specs=[pl.BlockSpec(memory_space=pltpu.SMEM), step_rows, stream, stream],
        out_specs=step_rows,
        compiler_params=_params(("parallel", "arbitrary")),
        name="attention",
    )(sink, q, k_dup, v_dup)


def _cumsum_rows(tri2, x):
    hi = x.astype(BF16)
    lo = (x - hi.astype(F32)).astype(BF16)
    return _dot(tri2, jnp.concatenate([hi, lo], axis=0))


def _hgrn_exact_chunk(reverse, rows, slot, q_ref, k_ref, lf_ref, v_ref, o_ref, old_ref, new_ref, row_ref, acc_ref):
    n = HGRN_CHUNK
    rr = lax.broadcasted_iota(jnp.int32, (n, n), 0)
    cc = lax.broadcasted_iota(jnp.int32, (n, n), 1)
    tri = jnp.where((cc >= rr) if reverse else (cc <= rr), 1.0, 0.0).astype(BF16)
    tri2 = jnp.concatenate([tri, tri], axis=1)
    last_row = 0 if reverse else n - 1
    s_idx = lax.broadcasted_iota(jnp.int32, (n, 1), 0)

    def batch_row(bi, carry):
        g_all = _cumsum_rows(tri2, lf_ref[bi, rows, :])
        for hd in range(HGRN_HEADS):
            ln = slice(hd * HGRN_DK, (hd + 1) * HGRN_DK)
            g = g_all[:, ln]
            q = q_ref[bi, rows, ln].astype(F32)
            k = k_ref[bi, rows, ln].astype(F32)
            v = v_ref[bi, rows, ln]
            vf = v.astype(F32)
            st = old_ref[bi, slot, hd]
            o_inter = _dot_nt((q * jnp.exp2(g)).astype(BF16), st.astype(BF16))
            row_ref[0] = g
            row_ref[1] = q

            def row(t, c):
                g_t = row_ref[0, pl.ds(t, 1), :]
                q_t = row_ref[1, pl.ds(t, 1), :]
                w = jnp.sum(q_t * k * jnp.exp2(jnp.minimum(g_t - g, 0.0)), axis=1, keepdims=True)
                before = (s_idx >= t) if reverse else (s_idx <= t)
                acc_ref[pl.ds(t, 1), :] = jnp.sum(jnp.where(before, w, 0.0) * vf, axis=0, keepdims=True)
                return c

            lax.fori_loop(0, n, row, 0)
            o_ref[bi, rows, ln] = (o_inter + acc_ref[...]).astype(o_ref.dtype)
            g_tot = g[last_row:last_row + 1, :]
            kd = (k * jnp.exp2(g_tot - g)).astype(BF16)
            new_ref[bi, slot, hd] = st * jnp.exp2(g_tot) + _dot_tn(v, kd)
        return carry

    lax.fori_loop(0, HGRN_BATCH, batch_row, 0)


def _hgrn_chunk(directions, old_ref, new_ref, row_ref, acc_ref):
    n = HGRN_CHUNK
    rr = lax.broadcasted_iota(jnp.int32, (n, n), 0)
    cc = lax.broadcasted_iota(jnp.int32, (n, n), 1)

    chains = []
    for slot, (reverse, rows, q_ref, k_ref, lf_ref, v_ref, o_ref) in enumerate(directions):
        causal = (cc >= rr) if reverse else (cc <= rr)
        last_row, mid_row = (0, n // 2) if reverse else (n - 1, n // 2 - 1)
        tri = jnp.where(causal, 1.0, 0.0).astype(BF16)
        tri2 = jnp.concatenate([tri, tri], axis=1)
        for bi in range(HGRN_BATCH):
            g_all = _cumsum_rows(tri2, lf_ref[bi, rows, :])
            for hd in range(HGRN_HEADS):
                ln = slice(hd * HGRN_DK, (hd + 1) * HGRN_DK)
                g = g_all[:, ln]
                chains.append(dict(
                    causal=causal, g=g, g_tot=g[last_row:last_row + 1, :], g_mid=g[mid_row:mid_row + 1, :],
                    q=q_ref[bi, rows, ln], k=k_ref[bi, rows, ln], v=v_ref[bi, rows, ln],
                    o_ref=o_ref, state=(bi, slot, hd), out=(bi, rows, ln)))

    spread = [None] * len(directions)
    for ch in chains:
        d = ch["g"] - ch["g_mid"]
        slot = ch["state"][1]
        spread[slot] = jnp.abs(d) if spread[slot] is None else jnp.maximum(spread[slot], jnp.abs(d))
        ch["qa"] = ch["q"] * jnp.exp2(jnp.minimum(d, EXP2_CLAMP)).astype(BF16)
        ch["kb"] = ch["k"] * jnp.exp2(jnp.minimum(-d, EXP2_CLAMP)).astype(BF16)
        ch["st_mid"] = old_ref[ch["state"]] * jnp.exp2(ch["g_mid"])
    for ch in chains:
        ch["a"] = _dot_nt(ch["qa"], ch["kb"])
        ch["o_inter"] = _dot_nt(ch["qa"], ch["st_mid"].astype(BF16))
        ch["u"] = _dot_tn(ch["v"], ch["kb"])
    for ch in chains:
        a = jnp.where(ch["causal"], ch["a"], 0.0).astype(BF16)
        ch["o_ref"][ch["out"]] = (ch["o_inter"] + _dot(a, ch["v"])).astype(ch["o_ref"].dtype)
        new_ref[ch["state"]] = (ch["st_mid"] + ch["u"]) * jnp.exp2(ch["g_tot"] - ch["g_mid"])

    @pl.when(jnp.max(functools.reduce(jnp.maximum, spread)) > EXP2_CLAMP)
    def _():
        for slot, (reverse, rows, q_ref, k_ref, lf_ref, v_ref, o_ref) in enumerate(directions):
            _hgrn_exact_chunk(reverse, rows, slot, q_ref, k_ref, lf_ref, v_ref, o_ref, old_ref, new_ref,
                              row_ref, acc_ref)


def _hgrn_kernel(qf_ref, kf_ref, lf_ref, vf_ref, qb_ref, kb_ref, lb_ref, vb_ref, of_ref, ob_ref,
                 state_ref, row_ref, acc_ref):
    @pl.when(pl.program_id(1) == 0)
    def _():
        state_ref[0] = jnp.zeros(state_ref.shape[1:], F32)

    first, second = slice(0, HGRN_CHUNK), slice(HGRN_CHUNK, 2 * HGRN_CHUNK)
    for sub in range(2):
        directions = ((False, (first, second)[sub], qf_ref, kf_ref, lf_ref, vf_ref, of_ref),
                      (True, (second, first)[sub], qb_ref, kb_ref, lb_ref, vb_ref, ob_ref))
        _hgrn_chunk(directions, state_ref.at[sub], state_ref.at[1 - sub], row_ref, acc_ref)


HGRN_STEP_ROWS = 2 * HGRN_CHUNK
HGRN_STEPS = TOKENS // HGRN_STEP_ROWS
HGRN_CTX_STEPS = CTX_LEN // HGRN_STEP_ROWS


def _hgrn(hq, hk, hlf, hv):
    bsz = hq.shape[0]
    assert bsz % HGRN_BATCH == 0
    lat = HGRN_STEPS - HGRN_CTX_STEPS

    def fwd(c):
        return jnp.where(c < HGRN_CTX_STEPS, lat + c, c - HGRN_CTX_STEPS)

    def bwd(c):
        return HGRN_STEPS - 1 - c

    def spec(order, lane_block):
        return pl.BlockSpec((HGRN_BATCH, HGRN_STEP_ROWS, BRANCH_WIDTH), lambda b, c: (b, order(c), lane_block))

    out = jax.ShapeDtypeStruct((bsz, TOKENS, BRANCH_WIDTH), BF16)
    return pl.pallas_call(
        _hgrn_kernel,
        out_shape=(out, out),
        grid=(bsz // HGRN_BATCH, HGRN_STEPS),
        in_specs=[spec(fwd, 0), spec(fwd, 0), spec(fwd, 0), spec(fwd, 0),
                  spec(bwd, 0), spec(bwd, 1), spec(bwd, 1), spec(bwd, 0)],
        out_specs=(spec(fwd, 0), spec(bwd, 0)),
        scratch_shapes=[pltpu.VMEM((2, HGRN_BATCH, 2, HGRN_HEADS, HGRN_DK, HGRN_DK), F32),
                        pltpu.VMEM((2, HGRN_CHUNK, HGRN_DK), F32), pltpu.VMEM((HGRN_CHUNK, HGRN_DK), F32)],
        compiler_params=_params(("parallel", "arbitrary")),
        name="hgrn",
    )(hq, hk, hlf, hv, hq, hk, hlf, hv)


FOURIER_ROW_TILE = 768
FOURIER_TILES = TOKENS // FOURIER_ROW_TILE
FOURIER_LAST_LATENT = SEQ - (FOURIER_TILES - 1) * FOURIER_ROW_TILE
assert FOURIER_TILES * FOURIER_ROW_TILE == TOKENS and FOURIER_LAST_LATENT + CTX_LEN == FOURIER_ROW_TILE


HALF_SEQ = SEQ // 2


def _fourier_kernel(x_ref, xc_ref, wc_ref, cs_ref, csc_ref, o_ref, uv_ref, nyq_ref):
    t = pl.program_id(1)
    last = FOURIER_TILES - 1
    latent_scale = 1.0 / math.sqrt(SEQ * FOURIER_GROUP_DIM)
    blk = LANES

    @pl.when(t == 0)
    def _():
        rr = lax.broadcasted_iota(jnp.int32, (blk, blk), 0)
        cc = lax.broadcasted_iota(jnp.int32, (blk, blk), 1)
        flip = jnp.where(rr + cc == blk, 1.0, 0.0).astype(BF16)
        first_row = lax.broadcasted_iota(jnp.int32, (blk, 1), 0) == 0
        n_blk = HALF_SEQ // blk
        cos_c = wc_ref[:, :BRANCH_WIDTH]
        sin_c = wc_ref[:, BRANCH_WIDTH:]
        for a in range(n_blk):
            src = HALF_SEQ + (n_blk - 1 - a) * blk
            z = _dot(flip, x_ref[0, src:src + blk, :])
            if a > 0:
                z = jnp.where(first_row, x_ref[0, src + blk:src + blk + 16, :].astype(F32)[0:1, :], z)
            lo = x_ref[0, a * blk:(a + 1) * blk, :].astype(F32)
            uv_ref[a * blk:(a + 1) * blk, :] = _dot((lo + z).astype(BF16), cos_c).astype(BF16)
            uv_ref[HALF_SEQ + a * blk:HALF_SEQ + (a + 1) * blk, :] = _dot((lo - z).astype(BF16), sin_c).astype(BF16)
        nyq_ref[...] = _dot(x_ref[0, HALF_SEQ:HALF_SEQ + nyq_ref.shape[0], :], cos_c)

    def latent_rows(r0, n):
        sign = 1.0 - 2.0 * ((r0 + lax.broadcasted_iota(jnp.int32, (n, 1), 0)) & 1).astype(F32)
        return (_dot(cs_ref[pl.ds(r0, n), :], uv_ref[...]) + sign * nyq_ref[0:1, :]) * latent_scale

    @pl.when(t < last)
    def _():
        r0 = pl.multiple_of(t * FOURIER_ROW_TILE, FOURIER_ROW_TILE)
        o_ref[0] = latent_rows(r0, FOURIER_ROW_TILE).astype(BF16)

    @pl.when(t == last)
    def _():
        o_ref[0, 0:FOURIER_LAST_LATENT, :] = latent_rows(last * FOURIER_ROW_TILE, FOURIER_LAST_LATENT).astype(BF16)
        uv = _dot(xc_ref[0], wc_ref[...])
        uvc = jnp.concatenate([uv[:, :BRANCH_WIDTH], uv[:, BRANCH_WIDTH:]], axis=0).astype(BF16)
        yc = _dot(csc_ref[...], uvc)
        o_ref[0, FOURIER_LAST_LATENT:FOURIER_ROW_TILE, :] = (
            yc * (1.0 / math.sqrt(CTX_LEN * FOURIER_GROUP_DIM))).astype(BF16)


def _fourier(four, wc, cs_lat, cs_ctx):
    bsz = four.shape[0]
    return pl.pallas_call(
        _fourier_kernel,
        out_shape=jax.ShapeDtypeStruct((bsz, TOKENS, BRANCH_WIDTH), BF16),
        grid=(bsz, FOURIER_TILES),
        in_specs=[pl.BlockSpec((1, SEQ, BRANCH_WIDTH), lambda b, t: (b, 0, 0)),
                  pl.BlockSpec((1, CTX_LEN, BRANCH_WIDTH), lambda b, t: (b, SEQ // CTX_LEN, 0)),
                  _const_spec(wc.shape), _const_spec(cs_lat.shape), _const_spec(cs_ctx.shape)],
        out_specs=pl.BlockSpec((1, FOURIER_ROW_TILE, BRANCH_WIDTH), lambda b, t: (b, t, 0)),
        scratch_shapes=[pltpu.VMEM((2 * HALF_SEQ, BRANCH_WIDTH), BF16), pltpu.VMEM((16, BRANCH_WIDTH), F32)],
        compiler_params=_params(("parallel", "arbitrary")),
        name="fourier",
    )(four, four, wc, cs_lat, cs_ctx)


def _dft_tables(n, split, n_cols):
    col = jnp.arange(n_cols, dtype=jnp.int32)

    def trig(row_factor):
        ang = ((row_factor[:, None] * col[None, :]) % n).astype(F32) * (2.0 * math.pi / n)
        return jnp.cos(ang), jnp.sin(ang)

    cos_a, sin_a = trig(split * jnp.arange(n // split, dtype=jnp.int32))
    cos_b, sin_b = trig(jnp.arange(split, dtype=jnp.int32))
    cos = cos_a[:, None, :] * cos_b[None, :, :] - sin_a[:, None, :] * sin_b[None, :, :]
    sin = sin_a[:, None, :] * cos_b[None, :, :] + cos_a[:, None, :] * sin_b[None, :, :]
    return jnp.concatenate([cos.reshape(n, n_cols), -sin.reshape(n, n_cols)], axis=1).astype(BF16)


def _channel_dft_table():
    c = jnp.arange(BRANCH_WIDTH, dtype=jnp.int32)
    same = (c[:, None] // FOURIER_GROUP_DIM) == (c[None, :] // FOURIER_GROUP_DIM)
    ang = (((c[:, None] % FOURIER_GROUP_DIM) * (c[None, :] % FOURIER_GROUP_DIM)) % FOURIER_GROUP_DIM
           ).astype(F32) * (2.0 * math.pi / FOURIER_GROUP_DIM)
    cos = jnp.where(same, jnp.cos(ang), 0.0)
    sin = jnp.where(same, jnp.sin(ang), 0.0)
    return jnp.concatenate([cos, sin], axis=1).astype(BF16)


FF_CHUNK = 1024


def _rms(x):
    return x * lax.rsqrt(jnp.mean(x * x, axis=-1, keepdims=True) + EPS)


def _merge_mlp_kernel(x_ref, mod_ref, ctx_mod_ref, of_ref, oa_ref, hf_ref, hb_ref, hg_ref, gate_ref,
                      hng_ref, g2_ref, wb_ref, wo_ref, w1_ref, w2_ref, out_ref):
    d = D_MODEL
    tile_rows = x_ref.shape[1]
    groups = [dict(rows=rows) for rows in _row_groups(tile_rows)]

    def mix_stage(st):
        rows = st["rows"]
        mods = _group_mods(mod_ref, ctx_mod_ref, tile_rows, rows)
        mod = st["mod"] = lambda i: mods[:, i * d:(i + 1) * d]
        o_h = hf_ref[0, rows, :].astype(F32) + hb_ref[0, rows, :].astype(F32)
        o_n = jnp.concatenate(
            [_rms(o_h[:, hd * HGRN_DK:(hd + 1) * HGRN_DK]) * hng_ref[...] for hd in range(HGRN_HEADS)], axis=-1)
        o_hr = (o_n * _silu(hg_ref[0, rows, :].astype(F32))).astype(BF16)
        branches = (of_ref[0, rows, :], oa_ref[0, rows, :], o_hr)
        mix = None
        for n, br in enumerate(branches):
            term = _sigmoid(gate_ref[0, rows, n * d:(n + 1) * d].astype(F32)) * _dot(br, wb_ref[n])
            mix = term if mix is None else mix + term
        y = _dot(mix.astype(BF16), wo_ref[...])
        st["x1"] = x1 = x_ref[0, rows, :] + mod(2) * y
        st["h2"] = (_rms(x1) * (g2_ref[...] * (1.0 + mod(4))) + mod(3)).astype(BF16)

    def mlp_stage(st):
        ff = None
        for c in range(D_FF // FF_CHUNK):
            cols = slice(c * FF_CHUNK, (c + 1) * FF_CHUNK)
            u = jnp.square(jnp.maximum(_dot(st["h2"], w1_ref[:, cols]), 0.0)).astype(BF16)
            part = _dot(u, w2_ref[cols, :])
            ff = part if ff is None else ff + part
        out_ref[0, st["rows"], :] = st["x1"] + st["mod"](5) * ff

    for stage in (mix_stage, mlp_stage):
        for st in groups:
            stage(st)


def _merge_mlp(xs, mods, o_four, o_attn, o_hf, o_hb, hg, gate, hng, g2, wb, wo, w1, w2, *, layer, latent_only):
    bsz = xs.shape[0]
    rows = LATENT_ROW_TILE if latent_only else ROW_TILE
    n_rows = SEQ if latent_only else TOKENS
    row = lambda w: pl.BlockSpec((1, rows, w), lambda b, t: (b, t, 0))
    return pl.pallas_call(
        _merge_mlp_kernel,
        out_shape=jax.ShapeDtypeStruct((bsz, n_rows, D_MODEL), F32),
        grid=(bsz, n_rows // rows),
        in_specs=[row(D_MODEL),
                  *_mod_specs(layer),
                  row(BRANCH_WIDTH), row(BRANCH_WIDTH), row(BRANCH_WIDTH), row(BRANCH_WIDTH),
                  row(BRANCH_WIDTH), row(N_BRANCH * D_MODEL),
                  *[_layer_spec(a.shape, layer) for a in (hng, g2, wb, wo, w1, w2)]],
        out_specs=row(D_MODEL),
        compiler_params=_params(("parallel", "parallel")),
        name="merge_mlp",
    )(xs, mods, mods, o_four, o_attn, o_hf, o_hb, hg, gate, hng, g2, wb, wo, w1, w2)


def _rope_tables():
    pos = jnp.arange(SEQ)
    row = (pos // GRID_W).astype(F32)
    col = (pos % GRID_W).astype(F32)
    axis_dim = HEAD_DIM // 2
    half = axis_dim // 2
    inv_freq = ROPE_THETA ** (-jnp.arange(0, axis_dim, 2, dtype=F32) / axis_dim)
    ang_r = row[:, None] * inv_freq
    ang_c = col[:, None] * inv_freq
    ang = jnp.concatenate([ang_r, ang_r, ang_c, ang_c], axis=1)
    sign = jnp.tile(jnp.concatenate([-jnp.ones(half, F32), jnp.ones(half, F32)]), 2)
    cos = jnp.concatenate([jnp.cos(ang), jnp.ones((CTX_LEN, HEAD_DIM), F32)], axis=0)
    sin = jnp.concatenate([jnp.sin(ang) * sign, jnp.zeros((CTX_LEN, HEAD_DIM), F32)], axis=0)
    return jnp.tile(cos, (1, 2)), jnp.tile(sin, (1, 2))


def kernel(x, c, ctx, c_ctx, w_mod, b_mod, norm1_g, norm2_g, w_in, q_norm_g, k_norm_g, attn_sink,
           hgrn_lb_logits, hgrn_norm_g, w_branch, w_out, w_ff1, w_ff2):
    bsz, n_tok, d = x.shape
    depth = w_mod.shape[0]
    assert (n_tok, d, ctx.shape[1]) == (SEQ, D_MODEL, CTX_LEN) and bsz <= MOD_ROWS // 2

    xs = jnp.concatenate([x, ctx], axis=1)
    cond = jnp.zeros((MOD_ROWS, d), F32).at[:bsz].set(c).at[MOD_ROWS // 2].set(c_ctx)
    mods = _modulation(cond, w_mod, b_mod).reshape(depth, MOD_ROWS, 1, 6 * d)

    cos, sin = _rope_tables()
    wc = _channel_dft_table()
    cs_lat = _dft_tables(SEQ, 32, HALF_SEQ)
    cs_ctx = _dft_tables(CTX_LEN, 16, CTX_LEN)
    lb_logits = hgrn_lb_logits.reshape(depth * 2, BRANCH_WIDTH)

    g1 = norm1_g.reshape(depth, 1, d)
    g2 = norm2_g.reshape(depth, 1, d)
    qg = jnp.tile(q_norm_g, (1, 2)).reshape(depth, 1, LANES)
    kg = jnp.tile(k_norm_g, (1, 2)).reshape(depth, 1, LANES)
    hng = hgrn_norm_g.reshape(depth, 1, HGRN_DK)
    w_in, w_branch, w_out, w_ff1, w_ff2 = (w.astype(BF16) for w in (w_in, w_branch, w_out, w_ff1, w_ff2))

    for l in range(depth):
        q, k_dup, v_dup, hv, hk, hlf, hq, hg, four, gate = _inproj(
            xs, mods, g1, w_in, qg, kg, cos, sin, lb_logits, layer=l)
        o_attn = _attention(q, k_dup, v_dup, attn_sink[l])
        o_hf, o_hb = _hgrn(hq, hk, hlf, hv)
        o_four = _fourier(four, wc, cs_lat, cs_ctx)
        xs = _merge_mlp(xs, mods, o_four, o_attn, o_hf, o_hb, hg, gate, hng, g2,
                        w_branch, w_out, w_ff1, w_ff2, layer=l, latent_only=(l == depth - 1))
    return xs
```

```python
import functools
import math

import jax
import jax.numpy as jnp
from jax import lax
from jax.experimental import pallas as pl
from jax.experimental.pallas import tpu as pltpu

F32 = jnp.float32
BF16 = jnp.bfloat16

D_MODEL = 1024
SEQ = 2048
CTX_LEN = 256
TOKENS = SEQ + CTX_LEN
GRID_W = 64
EPS = 1e-6
NEG_BIG = -1e30
LB_FLOOR = 1e-30
ROPE_THETA = 10000.0

BRANCH_WIDTH = D_MODEL // 2
HEAD_DIM = 64
ATTN_HEADS = BRANCH_WIDTH // HEAD_DIM
ATTN_KV_HEADS = ATTN_HEADS // 4
KV_WIDTH = ATTN_KV_HEADS * HEAD_DIM
ATTN_BLOCK = 128
FOURIER_GROUPS = 4
FOURIER_GROUP_DIM = BRANCH_WIDTH // FOURIER_GROUPS
HGRN_HEADS = 4
HGRN_DK = BRANCH_WIDTH // HGRN_HEADS
D_FF = 4 * D_MODEL
N_BRANCH = 3

_SEG = (KV_WIDTH, KV_WIDTH, BRANCH_WIDTH, BRANCH_WIDTH, BRANCH_WIDTH,
        BRANCH_WIDTH, BRANCH_WIDTH, BRANCH_WIDTH, BRANCH_WIDTH, N_BRANCH * D_MODEL)
_OFF = [sum(_SEG[:i]) for i in range(len(_SEG) + 1)]
D_IN = _OFF[-1]

LANES = 128
VMEM_LIMIT_BYTES = 56 * 1024 * 1024

ROW_TILE = 576
ROW_TILES = TOKENS // ROW_TILE
LATENT_ROW_TILE = 512
assert ROW_TILES * ROW_TILE == TOKENS and SEQ % LATENT_ROW_TILE == 0
HGRN_CHUNK = 64
HGRN_BATCH = 8
EXP2_CLAMP = 100.0
LOG2E = 1.0 / math.log(2.0)
MOD_ROWS = 16


def _const_spec(shape):
    nd = len(shape)
    return pl.BlockSpec(shape, lambda *_: (0,) * nd, pipeline_mode=pl.Buffered(1))


def _layer_spec(stacked_shape, layer):
    rest = tuple(stacked_shape[1:])
    return pl.BlockSpec((None,) + rest, lambda *_: (layer,) + (0,) * len(rest), pipeline_mode=pl.Buffered(1))


def _params(semantics, flags=None):
    return pltpu.CompilerParams(dimension_semantics=semantics, vmem_limit_bytes=VMEM_LIMIT_BYTES,
                                flags=flags)


def _dot(a, b):
    return jnp.dot(a, b, preferred_element_type=F32)


def _dot_nt(a, b):
    return lax.dot_general(a, b, (((1,), (1,)), ((), ())), preferred_element_type=F32)


def _dot_tn(a, b):
    return lax.dot_general(a, b, (((0,), (0,)), ((), ())), preferred_element_type=F32)


def _sigmoid(x):
    return 0.5 * jnp.tanh(0.5 * x) + 0.5


def _silu(x):
    return x * _sigmoid(x)


def _mod_specs(layer):
    one = lambda index: pl.BlockSpec((None, None, 1, 6 * D_MODEL), index)
    return [one(lambda b, t: (layer, b, 0, 0)), one(lambda b, t: (layer, MOD_ROWS // 2, 0, 0))]


def _row_groups(tile_rows, halve_first=False):
    split = SEQ % tile_rows or tile_rows // 2
    first = (slice(0, split // 2), slice(split // 2, split)) if halve_first else (slice(0, split),)
    return first + (slice(split, tile_rows),)


def _group_mods(mod_ref, ctx_mod_ref, tile_rows, rows):
    first_row = pl.program_id(1) * tile_rows + rows.start
    return jnp.where(first_row >= SEQ, ctx_mod_ref[...], mod_ref[...])


MOD_COL_TILE = 1536


def _modulation_kernel(cond_ref, w_ref, b_ref, out_ref):
    a = _silu(cond_ref[...]).astype(BF16)
    out_ref[...] = _dot(a, w_ref[...].astype(BF16)) + b_ref[...]


def _modulation(cond, w_mod, b_mod):
    depth = w_mod.shape[0]
    n = w_mod.shape[2]
    return pl.pallas_call(
        _modulation_kernel,
        out_shape=jax.ShapeDtypeStruct((depth, MOD_ROWS, n), F32),
        grid=(depth, n // MOD_COL_TILE),
        in_specs=[
            pl.BlockSpec((MOD_ROWS, D_MODEL), lambda l, j: (0, 0)),
            pl.BlockSpec((None, D_MODEL, MOD_COL_TILE), lambda l, j: (l, 0, j)),
            pl.BlockSpec((None, 1, MOD_COL_TILE), lambda l, j: (l, 0, j)),
        ],
        out_specs=pl.BlockSpec((None, MOD_ROWS, MOD_COL_TILE), lambda l, j: (l, 0, j)),
        compiler_params=_params(("parallel", "parallel")),
        name="modulation",
    )(cond, w_mod, b_mod.reshape(depth, 1, n))


def _lane_index(shape):
    return lax.broadcasted_iota(jnp.int32, shape, len(shape) - 1)


def _head_mean_matrix():
    shift = HEAD_DIM.bit_length() - 1
    r = lax.broadcasted_iota(jnp.int32, (LANES, LANES), 0) >> shift
    c = lax.broadcasted_iota(jnp.int32, (LANES, LANES), 1) >> shift
    return jnp.where(r == c, 1.0 / HEAD_DIM, 0.0).astype(BF16)


def _head_rms_norm(u, gain, mean_bd):
    ms = _dot((u * u).astype(BF16), mean_bd)
    return u * lax.rsqrt(ms + EPS) * gain


def _rope(u, cos, sin_signed):
    first = (_lane_index(u.shape) & 31) < 16
    partner = jnp.where(first, pltpu.roll(u, LANES - 16, axis=1), pltpu.roll(u, 16, axis=1))
    return u * cos + partner * sin_signed


def _hgrn_lower_bound(lb_ref, layer, direction):
    depth = lb_ref.shape[0] // 2
    rows = [lb_ref[2 * j + direction:2 * j + direction + 1, :] for j in range(depth)]
    m = functools.reduce(jnp.maximum, rows)
    e = [jnp.exp(r - m) for r in rows]
    z = functools.reduce(lambda a, b: a + b, e)
    p = [x / z for x in e]
    csum = functools.reduce(lambda a, b: a + b, p[:layer + 1])
    return csum - p[0]


def _hgrn_forget(z, lb):
    t = jnp.exp(-jnp.abs(z))
    pos = z >= 0.0
    lb_floor = jnp.maximum(lb, LB_FLOOR)
    r = 1.0 / (1.0 + t)
    tr = t * r
    k = (1.0 - lb) * jnp.where(pos, tr, r)
    f = lb_floor + (1.0 - lb_floor) * jnp.where(pos, r, tr)
    return k, jnp.log(f) * LOG2E


GATE_CHUNKS = 6


def _inproj_kernel(x_ref, mod_ref, ctx_mod_ref, g1_ref, w_ref, qg_ref, kg_ref, cos_ref, sin_ref, lb_ref,
                   q_ref, k_ref, v_ref, hv_ref, hk_ref, hlf_ref, hq_ref, hg_ref, four_ref, gate_ref,
                   *, layer):
    tile_rows = x_ref.shape[1]
    groups = [dict(rows=rows) for rows in _row_groups(tile_rows, halve_first=True)]
    mean_bd = _head_mean_matrix()
    q_gain = qg_ref[...] * (HEAD_DIM ** -0.5 * LOG2E)
    gate_width = N_BRANCH * D_MODEL // GATE_CHUNKS

    for st in groups:
        rows = st["rows"]
        mod = _group_mods(mod_ref, ctx_mod_ref, tile_rows, rows)
        x = x_ref[0, rows, :]
        ms = jnp.mean(x * x, axis=-1, keepdims=True)
        row_gain = g1_ref[...] * (1.0 + mod[:, D_MODEL:2 * D_MODEL])
        st["h"] = (x * lax.rsqrt(ms + EPS) * row_gain + mod[:, 0:D_MODEL]).astype(BF16)
        st["cos"] = cos_ref[rows, :]
        st["sin"] = sin_ref[rows, :]
        st["low_head"] = _lane_index((rows.stop - rows.start, LANES)) < HEAD_DIM

    def proj(st, seg, width=None):
        a = _OFF[seg]
        b = _OFF[seg + 1] if width is None else a + width
        return _dot(st["h"], w_ref[:, a:b])

    def gate_chunk(st, i):
        a = _OFF[9] + i * gate_width
        gate_ref[0, st["rows"], i * gate_width:(i + 1) * gate_width] = _dot(
            st["h"], w_ref[:, a:a + gate_width]).astype(BF16)

    for st in groups:
        st["kv"] = proj(st, 0, 2 * KV_WIDTH)
        gate_chunk(st, 0)
    for st in groups:
        rows, low_head, kv = st["rows"], st["low_head"], st.pop("kv")
        kr = _rope(_head_rms_norm(kv[:, :KV_WIDTH], kg_ref[...], mean_bd), st["cos"], st["sin"])
        kr_sw = pltpu.roll(kr, HEAD_DIM, axis=1)
        k_ref[0, 0, rows, :] = jnp.where(low_head, kr, kr_sw).astype(BF16)
        k_ref[0, 1, rows, :] = jnp.where(low_head, kr_sw, kr).astype(BF16)
        vr = kv[:, KV_WIDTH:]
        vr_sw = pltpu.roll(vr, HEAD_DIM, axis=1)
        v_ref[0, 0, rows, :] = jnp.where(low_head, vr, vr_sw).astype(BF16)
        v_ref[0, 1, rows, :] = jnp.where(low_head, vr_sw, vr).astype(BF16)

    for st in groups:
        st["q"] = proj(st, 5)
        gate_chunk(st, 1)
    for st in groups:
        qr = st.pop("q")
        for c in range(BRANCH_WIDTH // LANES):
            u = qr[:, c * LANES:(c + 1) * LANES]
            q_ref[0, st["rows"], c * LANES:(c + 1) * LANES] = _rope(
                _head_rms_norm(u, q_gain, mean_bd), st["cos"], st["sin"]).astype(BF16)

    for direction, seg in ((0, 3), (1, 4)):
        lb = _hgrn_lower_bound(lb_ref, layer, direction)
        cols = slice(direction * BRANCH_WIDTH, (direction + 1) * BRANCH_WIDTH)
        for st in groups:
            st["z"] = proj(st, seg)
            gate_chunk(st, 2 + direction)
        for st in groups:
            k, log2f = _hgrn_forget(st.pop("z"), lb)
            hk_ref[0, st["rows"], cols] = k.astype(BF16)
            hlf_ref[0, st["rows"], cols] = log2f
    for st in groups:
        st["qh"] = proj(st, 6)
        gate_chunk(st, 4)
    for st in groups:
        hq_ref[0, st["rows"], :] = _silu(st.pop("qh")).astype(BF16)
    for st in groups:
        rows = st["rows"]
        hv_ref[0, rows, :] = proj(st, 2).astype(BF16)
        hg_ref[0, rows, :] = proj(st, 7).astype(BF16)
        four_ref[0, rows, :] = proj(st, 8).astype(BF16)
        gate_chunk(st, 5)


def _inproj(xs, mods, g1, w_in, qg, kg, cos, sin, lb_logits, *, layer):
    bsz = xs.shape[0]
    row = lambda w: pl.BlockSpec((1, ROW_TILE, w), lambda b, t: (b, t, 0))
    dup = pl.BlockSpec((1, ATTN_KV_HEADS, ROW_TILE, LANES), lambda b, t: (b, 0, t, 0))
    tok = lambda w, dt: jax.ShapeDtypeStruct((bsz, TOKENS, w), dt)
    dup_shape = jax.ShapeDtypeStruct((bsz, ATTN_KV_HEADS, TOKENS, LANES), BF16)
    return pl.pallas_call(
        functools.partial(_inproj_kernel, layer=layer),
        out_shape=(tok(BRANCH_WIDTH, BF16), dup_shape, dup_shape,
                   tok(BRANCH_WIDTH, BF16), tok(2 * BRANCH_WIDTH, BF16), tok(2 * BRANCH_WIDTH, F32),
                   tok(BRANCH_WIDTH, BF16), tok(BRANCH_WIDTH, BF16), tok(BRANCH_WIDTH, BF16),
                   tok(N_BRANCH * D_MODEL, BF16)),
        grid=(bsz, ROW_TILES),
        in_specs=[
            row(D_MODEL),
            *_mod_specs(layer),
            _layer_spec(g1.shape, layer),
            _layer_spec(w_in.shape, layer),
            _layer_spec(qg.shape, layer),
            _layer_spec(kg.shape, layer),
            pl.BlockSpec((ROW_TILE, LANES), lambda b, t: (t, 0)),
            pl.BlockSpec((ROW_TILE, LANES), lambda b, t: (t, 0)),
            _const_spec(lb_logits.shape),
        ],
        out_specs=(row(BRANCH_WIDTH), dup, dup,
                   row(BRANCH_WIDTH), row(2 * BRANCH_WIDTH), row(2 * BRANCH_WIDTH),
                   row(BRANCH_WIDTH), row(BRANCH_WIDTH), row(BRANCH_WIDTH),
                   row(N_BRANCH * D_MODEL)),
        compiler_params=_params(("parallel", "parallel")),
        name="inproj",
    )(xs, mods, mods, g1, w_in, qg, kg, cos, sin, lb_logits)


ATTN_BLOCKS = TOKENS // ATTN_BLOCK
LATENT_ATTN_BLOCKS = SEQ // ATTN_BLOCK
GROUP = ATTN_HEADS // ATTN_KV_HEADS
PAIR_COLS = 2 * ATTN_BLOCK
ATTN_STEP_BLOCKS = 6
assert ATTN_BLOCKS % ATTN_STEP_BLOCKS == 0


def _attend_heads(sink_ref, q_ref, o_ref, query_blocks):
    low_head = _lane_index((ATTN_BLOCK, LANES)) < HEAD_DIM
    zero = jnp.zeros((), BF16)
    odd_cols = lax.broadcasted_iota(jnp.int32, (1, PAIR_COLS), 1) >= ATTN_BLOCK
    tasks = []
    for n, (kv_blocks, masks) in enumerate(query_blocks):
        kv = [kv_blocks(hk) for hk in range(ATTN_KV_HEADS)]
        rows = slice(n * ATTN_BLOCK, (n + 1) * ATTN_BLOCK)
        tasks += [dict(pair=c, rows=rows, kv=kv[(2 * c) // GROUP], masks=masks) for c in range(ATTN_HEADS // 2)]

    def scores_stage(st):
        c = st["pair"]
        tile = q_ref[0, st["rows"], c * LANES:(c + 1) * LANES]
        qs = jnp.concatenate([jnp.where(low_head, tile, zero), jnp.where(low_head, zero, tile)], axis=0)
        scores = [_dot_nt(kb, qs) for kb in st["kv"][0]]
        st["scores"] = [s if mk is None else jnp.where(mk, s, NEG_BIG) for s, mk in zip(scores, st["masks"])]
        st["sink"] = jnp.where(odd_cols, sink_ref[2 * c + 1], sink_ref[2 * c]) * LOG2E

    def max_stage(st):
        m = st["sink"]
        for s in st["scores"]:
            m = jnp.maximum(m, jnp.max(s, axis=0, keepdims=True))
        st["m"] = m

    def values_stage(st):
        denom = jnp.exp2(st["sink"] - st["m"])
        o = None
        for s, vb in zip(st["scores"], st["kv"][1]):
            p = jnp.exp2(s - st["m"])
            denom = denom + jnp.sum(p, axis=0, keepdims=True)
            pv = _dot_tn(vb, p.astype(BF16))
            o = pv if o is None else o + pv
        o = o / denom
        c = st["pair"]
        o_ref[0, st["rows"], c * LANES:(c + 1) * LANES] = jnp.where(
            low_head, o[:, :ATTN_BLOCK].T, o[:, ATTN_BLOCK:].T).astype(BF16)

    stages = (scores_stage, max_stage, values_stage)
    for tick in range(len(tasks) + len(stages) - 1):
        for depth, stage in enumerate(stages):
            if 0 <= tick - depth < len(tasks):
                stage(tasks[tick - depth])


def _attention_kernel(sink_ref, q_ref, k_ref, v_ref, o_ref):
    j2 = lax.broadcasted_iota(jnp.int32, (2 * ATTN_BLOCK, PAIR_COLS), 0)
    t2 = lax.broadcasted_iota(jnp.int32, (2 * ATTN_BLOCK, PAIR_COLS), 1) & (ATTN_BLOCK - 1)
    j1 = lax.broadcasted_iota(jnp.int32, (ATTN_BLOCK, PAIR_COLS), 0)
    t1 = lax.broadcasted_iota(jnp.int32, (ATTN_BLOCK, PAIR_COLS), 1) & (ATTN_BLOCK - 1)

    def query_block(n):
        i = pl.program_id(1) * ATTN_STEP_BLOCKS + n
        prev_start = pl.multiple_of(jnp.maximum(i - 1, 0) * ATTN_BLOCK, ATTN_BLOCK)
        cur_start = pl.multiple_of(i * ATTN_BLOCK, ATTN_BLOCK)
        next_start = pl.multiple_of(jnp.minimum(i + 1, LATENT_ATTN_BLOCKS - 1) * ATTN_BLOCK, ATTN_BLOCK)

        def blocks(ref, hk):
            near = jnp.concatenate([ref[0, hk, pl.ds(prev_start, ATTN_BLOCK), :],
                                    ref[0, hk, pl.ds(cur_start, ATTN_BLOCK), :]], axis=0)
            return near, ref[0, hk, pl.ds(next_start, ATTN_BLOCK), :], ref[0, hk, SEQ:TOKENS, :]

        in_range = jnp.logical_or(j2 >= ATTN_BLOCK, jnp.logical_and(j2 >= t2, i > 0))
        near_valid = jnp.logical_and(in_range, i < LATENT_ATTN_BLOCKS)
        next_valid = jnp.logical_and(j1 <= t1, i < LATENT_ATTN_BLOCKS - 1)
        return (lambda hk: (blocks(k_ref, hk), blocks(v_ref, hk))), (near_valid, next_valid, None)

    _attend_heads(sink_ref, q_ref, o_ref, [query_block(n) for n in range(ATTN_STEP_BLOCKS)])


def _attention(q, k_dup, v_dup, sink):
    bsz = q.shape[0]
    stream = pl.BlockSpec((1, ATTN_KV_HEADS, TOKENS, LANES), lambda b, i: (b, 0, 0, 0))
    step_rows = pl.BlockSpec((1, ATTN_STEP_BLOCKS * ATTN_BLOCK, BRANCH_WIDTH), lambda b, i: (b, i, 0))
    return pl.pallas_call(
        _attention_kernel,
        out_shape=jax.ShapeDtypeStruct((bsz, TOKENS, BRANCH_WIDTH), BF16),
        grid=(bsz, ATTN_BLOCKS // ATTN_STEP_BLOCKS),
        in_specs=[pl.BlockSpec(memory_space=pltpu.SMEM), step_rows, stream, stream],
        out_specs=step_rows,
        compiler_params=_params(("parallel", "arbitrary")),
        name="attention",
    )(sink, q, k_dup, v_dup)


def _cumsum_rows(tri2, x):
    hi = x.astype(BF16)
    lo = (x - hi.astype(F32)).astype(BF16)
    return _dot(tri2, jnp.concatenate([hi, lo], axis=0))


def _hgrn_exact_chunk(reverse, rows, slot, q_ref, k_ref, lf_ref, v_ref, o_ref, old_ref, new_ref, row_ref, acc_ref):
    n = HGRN_CHUNK
    rr = lax.broadcasted_iota(jnp.int32, (n, n), 0)
    cc = lax.broadcasted_iota(jnp.int32, (n, n), 1)
    tri = jnp.where((cc >= rr) if reverse else (cc <= rr), 1.0, 0.0).astype(BF16)
    tri2 = jnp.concatenate([tri, tri], axis=1)
    last_row = 0 if reverse else n - 1
    s_idx = lax.broadcasted_iota(jnp.int32, (n, 1), 0)

    def batch_row(bi, carry):
        g_all = _cumsum_rows(tri2, lf_ref[bi, rows, :])
        for hd in range(HGRN_HEADS):
            ln = slice(hd * HGRN_DK, (hd + 1) * HGRN_DK)
            g = g_all[:, ln]
            q = q_ref[bi, rows, ln].astype(F32)
            k = k_ref[bi, rows, ln].astype(F32)
            v = v_ref[bi, rows, ln]
            vf = v.astype(F32)
            st = old_ref[bi, slot, hd]
            o_inter = _dot_nt((q * jnp.exp2(g)).astype(BF16), st.astype(BF16))
            row_ref[0] = g
            row_ref[1] = q

            def row(t, c):
                g_t = row_ref[0, pl.ds(t, 1), :]
                q_t = row_ref[1, pl.ds(t, 1), :]
                w = jnp.sum(q_t * k * jnp.exp2(jnp.minimum(g_t - g, 0.0)), axis=1, keepdims=True)
                before = (s_idx >= t) if reverse else (s_idx <= t)
                acc_ref[pl.ds(t, 1), :] = jnp.sum(jnp.where(before, w, 0.0) * vf, axis=0, keepdims=True)
                return c

            lax.fori_loop(0, n, row, 0)
            o_ref[bi, rows, ln] = (o_inter + acc_ref[...]).astype(o_ref.dtype)
            g_tot = g[last_row:last_row + 1, :]
            kd = (k * jnp.exp2(g_tot - g)).astype(BF16)
            new_ref[bi, slot, hd] = st * jnp.exp2(g_tot) + _dot_tn(v, kd)
        return carry

    lax.fori_loop(0, HGRN_BATCH, batch_row, 0)


def _hgrn_chunk(directions, old_ref, new_ref, row_ref, acc_ref):
    n = HGRN_CHUNK
    rr = lax.broadcasted_iota(jnp.int32, (n, n), 0)
    cc = lax.broadcasted_iota(jnp.int32, (n, n), 1)

    chains = []
    for slot, (reverse, rows, q_ref, k_ref, lf_ref, v_ref, o_ref) in enumerate(directions):
        causal = (cc >= rr) if reverse else (cc <= rr)
        last_row, mid_row = (0, n // 2) if reverse else (n - 1, n // 2 - 1)
        tri = jnp.where(causal, 1.0, 0.0).astype(BF16)
        tri2 = jnp.concatenate([tri, tri], axis=1)
        for bi in range(HGRN_BATCH):
            g_all = _cumsum_rows(tri2, lf_ref[bi, rows, :])
            for hd in range(HGRN_HEADS):
                ln = slice(hd * HGRN_DK, (hd + 1) * HGRN_DK)
                g = g_all[:, ln]
                chains.append(dict(
                    causal=causal, g=g, g_tot=g[last_row:last_row + 1, :], g_mid=g[mid_row:mid_row + 1, :],
                    q=q_ref[bi, rows, ln], k=k_ref[bi, rows, ln], v=v_ref[bi, rows, ln],
                    o_ref=o_ref, state=(bi, slot, hd), out=(bi, rows, ln)))

    spread = [None] * len(directions)
    for ch in chains:
        d = ch["g"] - ch["g_mid"]
        slot = ch["state"][1]
        spread[slot] = jnp.abs(d) if spread[slot] is None else jnp.maximum(spread[slot], jnp.abs(d))
        ch["qa"] = ch["q"] * jnp.exp2(jnp.minimum(d, EXP2_CLAMP)).astype(BF16)
        ch["kb"] = ch["k"] * jnp.exp2(jnp.minimum(-d, EXP2_CLAMP)).astype(BF16)
        ch["st_mid"] = old_ref[ch["state"]] * jnp.exp2(ch["g_mid"])
    for ch in chains:
        ch["a"] = _dot_nt(ch["qa"], ch["kb"])
        ch["o_inter"] = _dot_nt(ch["qa"], ch["st_mid"].astype(BF16))
        ch["u"] = _dot_tn(ch["v"], ch["kb"])
    for ch in chains:
        a = jnp.where(ch["causal"], ch["a"], 0.0).astype(BF16)
        ch["o_ref"][ch["out"]] = (ch["o_inter"] + _dot(a, ch["v"])).astype(ch["o_ref"].dtype)
        new_ref[ch["state"]] = (ch["st_mid"] + ch["u"]) * jnp.exp2(ch["g_tot"] - ch["g_mid"])

    @pl.when(jnp.max(functools.reduce(jnp.maximum, spread)) > EXP2_CLAMP)
    def _():
        for slot, (reverse, rows, q_ref, k_ref, lf_ref, v_ref, o_ref) in enumerate(directions):
            _hgrn_exact_chunk(reverse, rows, slot, q_ref, k_ref, lf_ref, v_ref, o_ref, old_ref, new_ref,
                              row_ref, acc_ref)


def _hgrn_kernel(qf_ref, kf_ref, lf_ref, vf_ref, qb_ref, kb_ref, lb_ref, vb_ref, of_ref, ob_ref,
                 state_ref, row_ref, acc_ref):
    @pl.when(pl.program_id(1) == 0)
    def _():
        state_ref[0] = jnp.zeros(state_ref.shape[1:], F32)

    first, second = slice(0, HGRN_CHUNK), slice(HGRN_CHUNK, 2 * HGRN_CHUNK)
    for sub in range(2):
        directions = ((False, (first, second)[sub], qf_ref, kf_ref, lf_ref, vf_ref, of_ref),
                      (True, (second, first)[sub], qb_ref, kb_ref, lb_ref, vb_ref, ob_ref))
        _hgrn_chunk(directions, state_ref.at[sub], state_ref.at[1 - sub], row_ref, acc_ref)


HGRN_STEP_ROWS = 2 * HGRN_CHUNK
HGRN_STEPS = TOKENS // HGRN_STEP_ROWS
HGRN_CTX_STEPS = CTX_LEN // HGRN_STEP_ROWS


def _hgrn(hq, hk, hlf, hv):
    bsz = hq.shape[0]
    assert bsz % HGRN_BATCH == 0
    lat = HGRN_STEPS - HGRN_CTX_STEPS

    def fwd(c):
        return jnp.where(c < HGRN_CTX_STEPS, lat + c, c - HGRN_CTX_STEPS)

    def bwd(c):
        return HGRN_STEPS - 1 - c

    def spec(order, lane_block):
        return pl.BlockSpec((HGRN_BATCH, HGRN_STEP_ROWS, BRANCH_WIDTH), lambda b, c: (b, order(c), lane_block))

    out = jax.ShapeDtypeStruct((bsz, TOKENS, BRANCH_WIDTH), BF16)
    return pl.pallas_call(
        _hgrn_kernel,
        out_shape=(out, out),
        grid=(bsz // HGRN_BATCH, HGRN_STEPS),
        in_specs=[spec(fwd, 0), spec(fwd, 0), spec(fwd, 0), spec(fwd, 0),
                  spec(bwd, 0), spec(bwd, 1), spec(bwd, 1), spec(bwd, 0)],
        out_specs=(spec(fwd, 0), spec(bwd, 0)),
        scratch_shapes=[pltpu.VMEM((2, HGRN_BATCH, 2, HGRN_HEADS, HGRN_DK, HGRN_DK), F32),
                        pltpu.VMEM((2, HGRN_CHUNK, HGRN_DK), F32), pltpu.VMEM((HGRN_CHUNK, HGRN_DK), F32)],
        compiler_params=_params(("parallel", "arbitrary")),
        name="hgrn",
    )(hq, hk, hlf, hv, hq, hk, hlf, hv)


FOURIER_ROW_TILE = 768
FOURIER_TILES = TOKENS // FOURIER_ROW_TILE
FOURIER_LAST_LATENT = SEQ - (FOURIER_TILES - 1) * FOURIER_ROW_TILE
assert FOURIER_TILES * FOURIER_ROW_TILE == TOKENS and FOURIER_LAST_LATENT + CTX_LEN == FOURIER_ROW_TILE


HALF_SEQ = SEQ // 2


def _fourier_kernel(x_ref, xc_ref, wc_ref, cs_ref, csc_ref, o_ref, uv_ref, nyq_ref):
    t = pl.program_id(1)
    last = FOURIER_TILES - 1
    latent_scale = 1.0 / math.sqrt(SEQ * FOURIER_GROUP_DIM)
    blk = LANES

    @pl.when(t == 0)
    def _():
        rr = lax.broadcasted_iota(jnp.int32, (blk, blk), 0)
        cc = lax.broadcasted_iota(jnp.int32, (blk, blk), 1)
        flip = jnp.where(rr + cc == blk, 1.0, 0.0).astype(BF16)
        first_row = lax.broadcasted_iota(jnp.int32, (blk, 1), 0) == 0
        n_blk = HALF_SEQ // blk
        cos_c = wc_ref[:, :BRANCH_WIDTH]
        sin_c = wc_ref[:, BRANCH_WIDTH:]
        for a in range(n_blk):
            src = HALF_SEQ + (n_blk - 1 - a) * blk
            z = _dot(flip, x_ref[0, src:src + blk, :])
            if a > 0:
                z = jnp.where(first_row, x_ref[0, src + blk:src + blk + 16, :].astype(F32)[0:1, :], z)
            lo = x_ref[0, a * blk:(a + 1) * blk, :].astype(F32)
            uv_ref[a * blk:(a + 1) * blk, :] = _dot((lo + z).astype(BF16), cos_c).astype(BF16)
            uv_ref[HALF_SEQ + a * blk:HALF_SEQ + (a + 1) * blk, :] = _dot((lo - z).astype(BF16), sin_c).astype(BF16)
        nyq_ref[...] = _dot(x_ref[0, HALF_SEQ:HALF_SEQ + nyq_ref.shape[0], :], cos_c)

    def latent_rows(r0, n):
        sign = 1.0 - 2.0 * ((r0 + lax.broadcasted_iota(jnp.int32, (n, 1), 0)) & 1).astype(F32)
        return (_dot(cs_ref[pl.ds(r0, n), :], uv_ref[...]) + sign * nyq_ref[0:1, :]) * latent_scale

    @pl.when(t < last)
    def _():
        r0 = pl.multiple_of(t * FOURIER_ROW_TILE, FOURIER_ROW_TILE)
        o_ref[0] = latent_rows(r0, FOURIER_ROW_TILE).astype(BF16)

    @pl.when(t == last)
    def _():
        o_ref[0, 0:FOURIER_LAST_LATENT, :] = latent_rows(last * FOURIER_ROW_TILE, FOURIER_LAST_LATENT).astype(BF16)
        uv = _dot(xc_ref[0], wc_ref[...])
        uvc = jnp.concatenate([uv[:, :BRANCH_WIDTH], uv[:, BRANCH_WIDTH:]], axis=0).astype(BF16)
        yc = _dot(csc_ref[...], uvc)
        o_ref[0, FOURIER_LAST_LATENT:FOURIER_ROW_TILE, :] = (
            yc * (1.0 / math.sqrt(CTX_LEN * FOURIER_GROUP_DIM))).astype(BF16)


def _fourier(four, wc, cs_lat, cs_ctx):
    bsz = four.shape[0]
    return pl.pallas_call(
        _fourier_kernel,
        out_shape=jax.ShapeDtypeStruct((bsz, TOKENS, BRANCH_WIDTH), BF16),
        grid=(bsz, FOURIER_TILES),
        in_specs=[pl.BlockSpec((1, SEQ, BRANCH_WIDTH), lambda b, t: (b, 0, 0)),
                  pl.BlockSpec((1, CTX_LEN, BRANCH_WIDTH), lambda b, t: (b, SEQ // CTX_LEN, 0)),
                  _const_spec(wc.shape), _const_spec(cs_lat.shape), _const_spec(cs_ctx.shape)],
        out_specs=pl.BlockSpec((1, FOURIER_ROW_TILE, BRANCH_WIDTH), lambda b, t: (b, t, 0)),
        scratch_shapes=[pltpu.VMEM((2 * HALF_SEQ, BRANCH_WIDTH), BF16), pltpu.VMEM((16, BRANCH_WIDTH), F32)],
        compiler_params=_params(("parallel", "arbitrary")),
        name="fourier",
    )(four, four, wc, cs_lat, cs_ctx)


def _dft_tables(n, split, n_cols):
    col = jnp.arange(n_cols, dtype=jnp.int32)

    def trig(row_factor):
        ang = ((row_factor[:, None] * col[None, :]) % n).astype(F32) * (2.0 * math.pi / n)
        return jnp.cos(ang), jnp.sin(ang)

    cos_a, sin_a = trig(split * jnp.arange(n // split, dtype=jnp.int32))
    cos_b, sin_b = trig(jnp.arange(split, dtype=jnp.int32))
    cos = cos_a[:, None, :] * cos_b[None, :, :] - sin_a[:, None, :] * sin_b[None, :, :]
    sin = sin_a[:, None, :] * cos_b[None, :, :] + cos_a[:, None, :] * sin_b[None, :, :]
    return jnp.concatenate([cos.reshape(n, n_cols), -sin.reshape(n, n_cols)], axis=1).astype(BF16)


def _channel_dft_table():
    c = jnp.arange(BRANCH_WIDTH, dtype=jnp.int32)
    same = (c[:, None] // FOURIER_GROUP_DIM) == (c[None, :] // FOURIER_GROUP_DIM)
    ang = (((c[:, None] % FOURIER_GROUP_DIM) * (c[None, :] % FOURIER_GROUP_DIM)) % FOURIER_GROUP_DIM
           ).astype(F32) * (2.0 * math.pi / FOURIER_GROUP_DIM)
    cos = jnp.where(same, jnp.cos(ang), 0.0)
    sin = jnp.where(same, jnp.sin(ang), 0.0)
    return jnp.concatenate([cos, sin], axis=1).astype(BF16)


FF_CHUNK = 1024


def _rms(x):
    return x * lax.rsqrt(jnp.mean(x * x, axis=-1, keepdims=True) + EPS)


def _merge_mlp_kernel(x_ref, mod_ref, ctx_mod_ref, of_ref, oa_ref, hf_ref, hb_ref, hg_ref, gate_ref,
                      hng_ref, g2_ref, wb_ref, wo_ref, w1_ref, w2_ref, out_ref):
    d = D_MODEL
    tile_rows = x_ref.shape[1]
    groups = [dict(rows=rows) for rows in _row_groups(tile_rows)]

    def mix_stage(st):
        rows = st["rows"]
        mods = _group_mods(mod_ref, ctx_mod_ref, tile_rows, rows)
        mod = st["mod"] = lambda i: mods[:, i * d:(i + 1) * d]
        o_h = hf_ref[0, rows, :].astype(F32) + hb_ref[0, rows, :].astype(F32)
        o_n = jnp.concatenate(
            [_rms(o_h[:, hd * HGRN_DK:(hd + 1) * HGRN_DK]) * hng_ref[...] for hd in range(HGRN_HEADS)], axis=-1)
        o_hr = (o_n * _silu(hg_ref[0, rows, :].astype(F32))).astype(BF16)
        branches = (of_ref[0, rows, :], oa_ref[0, rows, :], o_hr)
        mix = None
        for n, br in enumerate(branches):
            term = _sigmoid(gate_ref[0, rows, n * d:(n + 1) * d].astype(F32)) * _dot(br, wb_ref[n])
            mix = term if mix is None else mix + term
        y = _dot(mix.astype(BF16), wo_ref[...])
        st["x1"] = x1 = x_ref[0, rows, :] + mod(2) * y
        st["h2"] = (_rms(x1) * (g2_ref[...] * (1.0 + mod(4))) + mod(3)).astype(BF16)

    def mlp_stage(st):
        ff = None
        for c in range(D_FF // FF_CHUNK):
            cols = slice(c * FF_CHUNK, (c + 1) * FF_CHUNK)
            u = jnp.square(jnp.maximum(_dot(st["h2"], w1_ref[:, cols]), 0.0)).astype(BF16)
            part = _dot(u, w2_ref[cols, :])
            ff = part if ff is None else ff + part
        out_ref[0, st["rows"], :] = st["x1"] + st["mod"](5) * ff

    for stage in (mix_stage, mlp_stage):
        for st in groups:
            stage(st)


def _merge_mlp(xs, mods, o_four, o_attn, o_hf, o_hb, hg, gate, hng, g2, wb, wo, w1, w2, *, layer, latent_only):
    bsz = xs.shape[0]
    rows = LATENT_ROW_TILE if latent_only else ROW_TILE
    n_rows = SEQ if latent_only else TOKENS
    row = lambda w: pl.BlockSpec((1, rows, w), lambda b, t: (b, t, 0))
    return pl.pallas_call(
        _merge_mlp_kernel,
        out_shape=jax.ShapeDtypeStruct((bsz, n_rows, D_MODEL), F32),
        grid=(bsz, n_rows // rows),
        in_specs=[row(D_MODEL),
                  *_mod_specs(layer),
                  row(BRANCH_WIDTH), row(BRANCH_WIDTH), row(BRANCH_WIDTH), row(BRANCH_WIDTH),
                  row(BRANCH_WIDTH), row(N_BRANCH * D_MODEL),
                  *[_layer_spec(a.shape, layer) for a in (hng, g2, wb, wo, w1, w2)]],
        out_specs=row(D_MODEL),
        compiler_params=_params(("parallel", "parallel")),
        name="merge_mlp",
    )(xs, mods, mods, o_four, o_attn, o_hf, o_hb, hg, gate, hng, g2, wb, wo, w1, w2)


def _rope_tables():
    pos = jnp.arange(SEQ)
    row = (pos // GRID_W).astype(F32)
    col = (pos % GRID_W).astype(F32)
    axis_dim = HEAD_DIM // 2
    half = axis_dim // 2
    inv_freq = ROPE_THETA ** (-jnp.arange(0, axis_dim, 2, dtype=F32) / axis_dim)
    ang_r = row[:, None] * inv_freq
    ang_c = col[:, None] * inv_freq
    ang = jnp.concatenate([ang_r, ang_r, ang_c, ang_c], axis=1)
    sign = jnp.tile(jnp.concatenate([-jnp.ones(half, F32), jnp.ones(half, F32)]), 2)
    cos = jnp.concatenate([jnp.cos(ang), jnp.ones((CTX_LEN, HEAD_DIM), F32)], axis=0)
    sin = jnp.concatenate([jnp.sin(ang) * sign, jnp.zeros((CTX_LEN, HEAD_DIM), F32)], axis=0)
    return jnp.tile(cos, (1, 2)), jnp.tile(sin, (1, 2))


def kernel(x, c, ctx, c_ctx, w_mod, b_mod, norm1_g, norm2_g, w_in, q_norm_g, k_norm_g, attn_sink,
           hgrn_lb_logits, hgrn_norm_g, w_branch, w_out, w_ff1, w_ff2):
    bsz, n_tok, d = x.shape
    depth = w_mod.shape[0]
    assert (n_tok, d, ctx.shape[1]) == (SEQ, D_MODEL, CTX_LEN) and bsz <= MOD_ROWS // 2

    xs = jnp.concatenate([x, ctx], axis=1)
    cond = jnp.zeros((MOD_ROWS, d), F32).at[:bsz].set(c).at[MOD_ROWS // 2].set(c_ctx)
    mods = _modulation(cond, w_mod, b_mod).reshape(depth, MOD_ROWS, 1, 6 * d)

    cos, sin = _rope_tables()
    wc = _channel_dft_table()
    cs_lat = _dft_tables(SEQ, 32, HALF_SEQ)
    cs_ctx = _dft_tables(CTX_LEN, 16, CTX_LEN)
    lb_logits = hgrn_lb_logits.reshape(depth * 2, BRANCH_WIDTH)

    g1 = norm1_g.reshape(depth, 1, d)
    g2 = norm2_g.reshape(depth, 1, d)
    qg = jnp.tile(q_norm_g, (1, 2)).reshape(depth, 1, LANES)
    kg = jnp.tile(k_norm_g, (1, 2)).reshape(depth, 1, LANES)
    hng = hgrn_norm_g.reshape(depth, 1, HGRN_DK)
    w_in, w_branch, w_out, w_ff1, w_ff2 = (w.astype(BF16) for w in (w_in, w_branch, w_out, w_ff1, w_ff2))

    for l in range(depth):
        q, k_dup, v_dup, hv, hk, hlf, hq, hg, four, gate = _inproj(
            xs, mods, g1, w_in, qg, kg, cos, sin, lb_logits, layer=l)
        o_attn = _attention(q, k_dup, v_dup, attn_sink[l])
        o_hf, o_hb = _hgrn(hq, hk, hlf, hv)
        o_four = _fourier(four, wc, cs_lat, cs_ctx)
        xs = _merge_mlp(xs, mods, o_four, o_attn, o_hf, o_hb, hg, gate, hng, g2,
                        w_branch, w_out, w_ff1, w_ff2, layer=l, latent_only=(l == depth - 1))
    return xs
```
